```python
import math
import jax, jax.numpy as jnp
from jax import lax
import numpy as np

D_MODEL = 1024
BATCH = 16
SEQ = 4096
DEPTH = 4

N_MIXERS = 2
N_RWKV_LAYERS = (DEPTH + 1) // 2
N_ATTN_LAYERS = DEPTH // 2

RWKV_HEAD_SIZE = 64
RWKV_HEADS = D_MODEL // RWKV_HEAD_SIZE
RWKV_DECAY_LORA = max(32, int(round(1.8 * D_MODEL ** 0.5 / 32)) * 32)
RWKV_AAA_LORA = max(32, int(round(1.8 * D_MODEL ** 0.5 / 32)) * 32)
RWKV_MV_LORA = max(32, int(round(1.3 * D_MODEL ** 0.5 / 32)) * 32)
RWKV_GATE_LORA = max(32, int(round(0.6 * D_MODEL ** 0.8 / 32)) * 32)
RWKV_LN_EPS = 64e-5

ATTN_HEAD_DIM = 64
ATTN_HEADS = D_MODEL // ATTN_HEAD_DIM
DILATION_PAIRS = ((128, 1), (512, 4), (2048, 16))
N_GROUPS = len(DILATION_PAIRS)
BAND_BLOCK = max(w // d for w, d in DILATION_PAIRS)
NUM_BUCKETS = 32
MAX_DISTANCE = max(w for w, _ in DILATION_PAIRS)
NEG_INF = -1e30

D_FF = -(-8 * D_MODEL // (3 * 256)) * 256
RMS_EPS = 1e-6

kernel_name = "hybrid_rwkv7_dilated_attn_trunk"


def rms_norm(x, gain, eps=RMS_EPS):
    xf = x.astype(jnp.float32)
    y = xf * lax.rsqrt(jnp.mean(xf * xf, axis=-1, keepdims=True) + eps)
    return (y * gain).astype(x.dtype)


def swiglu_ffn(h, w_in, w_out):
    gate, up = jnp.split(h @ w_in, 2, axis=-1)
    return (jax.nn.silu(gate) * up) @ w_out


def rwkv7_scan(r, decay, k, v, a_vec, b_vec):
    B, S, H, N = r.shape

    def step(state, inp):
        r_t, w_t, k_t, v_t, a_t, b_t = inp
        sa = jnp.einsum('bhij,bhj->bhi', state, a_t)
        state = (state * w_t[:, :, None, :] + sa[..., None] * b_t[:, :, None, :]
                 + v_t[..., None] * k_t[:, :, None, :])
        return state, jnp.einsum('bhij,bhj->bhi', state, r_t)

    xs = tuple(jnp.swapaxes(t, 0, 1) for t in (r, decay, k, v, a_vec, b_vec))
    _, y = lax.scan(step, jnp.zeros((B, H, N, N), jnp.float32), xs)
    return jnp.swapaxes(y, 0, 1)


def rwkv7_time_mix(h, v_first, mu, w_rkv, w0, w1, w2, a0, a1, a2, g1, g2,
                   k_k, k_a, r_k, ln_w, ln_b, w_o, v_gate):
    B, S, D = h.shape
    H, N = RWKV_HEADS, RWKV_HEAD_SIZE

    def heads(t):
        return t.reshape(B, S, H, N).astype(jnp.float32)

    xx = jnp.pad(h, ((0, 0), (1, 0), (0, 0)))[:, :-1] - h
    x_mix = h[None] + xx[None] * mu[:, None, None, :]
    rkv = jnp.einsum('cbsd,cde->cbse', x_mix[:3], w_rkv)
    r, k, v = rkv[0], rkv[1], rkv[2]
    xw, xa, xg, xv = x_mix[3], x_mix[4], x_mix[5], x_mix[2]

    w = -jax.nn.softplus(-(w0 + jnp.tanh(xw @ w1) @ w2)) - 0.5
    decay = jnp.exp(-jnp.exp(w.astype(jnp.float32)))
    a = jax.nn.sigmoid(a0 + (xa @ a1) @ a2)
    g = jax.nn.sigmoid(xg @ g1) @ g2
    if v_gate is None:
        v_first = v
    else:
        v0, v1, v2 = v_gate
        v = v + (v_first - v) * jax.nn.sigmoid(v0 + (xv @ v1) @ v2)

    kk = heads(k * k_k)
    kk = kk * lax.rsqrt(jnp.maximum(jnp.sum(kk * kk, axis=-1, keepdims=True), 1e-24))
    k = k * (1.0 + (a - 1.0) * k_a)
    rh, kh, vh, ah = heads(r), heads(k), heads(v), heads(a)
    y = rwkv7_scan(rh, decay.reshape(B, S, H, N), kh, vh, -kk, kk * ah)

    mean = jnp.mean(y, axis=-1, keepdims=True)
    var = jnp.mean(jnp.square(y - mean), axis=-1, keepdims=True)
    y = ((y - mean) * lax.rsqrt(var + RWKV_LN_EPS)).reshape(B, S, D) * ln_w + ln_b
    r_k_h = r_k.reshape(H, N).astype(jnp.float32)
    bonus = jnp.sum(rh * kh * r_k_h, axis=-1, keepdims=True) * vh
    out = ((y + bonus.reshape(B, S, D)) * g).astype(h.dtype) @ w_o
    return out, v_first


def t5_bucket(dist):
    max_exact = NUM_BUCKETS // 2
    large = max_exact + (jnp.log(jnp.maximum(dist, 1).astype(jnp.float32) / max_exact)
                         / math.log(MAX_DISTANCE / max_exact)
                         * (NUM_BUCKETS - max_exact)).astype(jnp.int32)
    large = jnp.minimum(large, NUM_BUCKETS - 1)
    return jnp.where(dist < max_exact, dist, large)


def band_bias(table_g, window, dilation):
    blk = BAND_BLOCK
    i = jnp.arange(blk)[:, None]
    c = jnp.arange(2 * blk)[None, :]
    steps = i + blk - c
    allowed = (steps >= 0) & (steps <= window // dilation)
    bucket = t5_bucket(jnp.maximum(steps, 0) * dilation)
    bias = jnp.take(table_g, bucket, axis=0).astype(jnp.float32)
    bias = jnp.where(allowed[..., None], bias, NEG_INF)
    return jnp.transpose(bias, (2, 0, 1))


def dilated_band_attention(q, k, v, bias, dilation):
    B, S, H, Dh = q.shape
    blk = BAND_BLOCK
    span = blk * dilation
    s_pad = -(-S // span) * span
    nb = s_pad // span

    def to_blocks(t):
        t = jnp.pad(t, ((0, 0), (0, s_pad - S), (0, 0), (0, 0)))
        return t.reshape(B, nb, blk, dilation, H, Dh)

    def band(t):
        prev = jnp.concatenate([jnp.zeros_like(t[:, :1]), t[:, :-1]], axis=1)
        return jnp.concatenate([prev, t], axis=2)

    qb = to_blocks(q)
    kband, vband = band(to_blocks(k)), band(to_blocks(v))
    logits = jnp.einsum('bnirhe,bncrhe->bnrhic', qb, kband).astype(jnp.float32)
    logits = logits + bias[None, None, None]
    key_ok = (jnp.arange(nb)[:, None] > 0) | (jnp.arange(2 * blk)[None, :] >= blk)
    logits = jnp.where(key_ok[None, :, None, None, None, :], logits, NEG_INF)
    lse = jax.nn.logsumexp(logits, axis=-1)
    probs = jnp.exp(logits - lse[..., None]).astype(v.dtype)
    out = jnp.einsum('bnrhic,bncrhe->bnirhe', probs, vband).reshape(B, s_pad, H, Dh)[:, :S]
    lse = jnp.transpose(lse, (0, 1, 4, 2, 3)).reshape(B, s_pad, H)[:, :S]
    return out, lse


def dilated_attention_mixer(h, w_qkv, q_gain, k_gain, w_o, rel_bias):
    B, S, _ = h.shape
    qkv = (h @ w_qkv).reshape(B, S, N_GROUPS, 3, ATTN_HEADS, ATTN_HEAD_DIM)
    scale = ATTN_HEAD_DIM ** -0.5
    outs, lses = [], []
    for g, (window, dilation) in enumerate(DILATION_PAIRS):
        q = rms_norm(qkv[:, :, g, 0], q_gain[g]) * scale
        k = rms_norm(qkv[:, :, g, 1], k_gain[g])
        v = qkv[:, :, g, 2]
        bias = band_bias(rel_bias[:, g * ATTN_HEADS:(g + 1) * ATTN_HEADS], window, dilation)
        o, l = dilated_band_attention(q, k, v, bias, dilation)
        outs.append(o)
        lses.append(l)
    weights = jax.nn.softmax(jnp.stack(lses), axis=0)
    o = jnp.sum(weights[..., None] * jnp.stack(outs).astype(jnp.float32), axis=0)
    return o.reshape(B, S, ATTN_HEADS * ATTN_HEAD_DIM).astype(h.dtype) @ w_o


def setup_inputs(seed: int = 0) -> dict:
    key = jax.random.key(seed)
    ks = iter(jax.random.split(key, 40))
    f32 = jnp.float32
    D, NR, NA = D_MODEL, N_RWKV_LAYERS, N_ATTN_LAYERS

    def nrm(shape, scale):
        return jax.random.normal(next(ks), shape, f32) * scale

    def gain(shape):
        return 1.0 + nrm(shape, 0.02)

    qkv_cols = N_GROUPS * 3 * ATTN_HEADS * ATTN_HEAD_DIM
    return {
        "x": nrm((BATCH, SEQ, D), 1.0),
        "mix_norm": gain((DEPTH, D)),
        "ffn_norm": gain((DEPTH, D)),
        "ffn_w_in": nrm((DEPTH, D, 2 * D_FF), D ** -0.5),
        "ffn_w_out": nrm((DEPTH, D_FF, D), D_FF ** -0.5),
        "rwkv_mu": jax.random.uniform(next(ks), (NR, 6, D), f32),
        "rwkv_w_rkv": nrm((NR, 3, D, D), D ** -0.5),
        "rwkv_w0": -6.0 + 5.0 * jax.random.uniform(next(ks), (NR, D), f32),
        "rwkv_w1": nrm((NR, D, RWKV_DECAY_LORA), D ** -0.5),
        "rwkv_w2": nrm((NR, RWKV_DECAY_LORA, D), 0.5 * RWKV_DECAY_LORA ** -0.5),
        "rwkv_a0": nrm((NR, D), 0.1),
        "rwkv_a1": nrm((NR, D, RWKV_AAA_LORA), D ** -0.5),
        "rwkv_a2": nrm((NR, RWKV_AAA_LORA, D), 0.5 * RWKV_AAA_LORA ** -0.5),
        "rwkv_v0": 1.0 + nrm((NR - 1, D), 0.1),
        "rwkv_v1": nrm((NR - 1, D, RWKV_MV_LORA), D ** -0.5),
        "rwkv_v2": nrm((NR - 1, RWKV_MV_LORA, D), 0.5 * RWKV_MV_LORA ** -0.5),
        "rwkv_g1": nrm((NR, D, RWKV_GATE_LORA), D ** -0.5),
        "rwkv_g2": nrm((NR, RWKV_GATE_LORA, D), RWKV_GATE_LORA ** -0.5),
        "rwkv_k_k": 0.85 + nrm((NR, D), 0.02),
        "rwkv_k_a": gain((NR, D)),
        "rwkv_r_k": nrm((NR, D), 0.1),
        "rwkv_ln_w": gain((NR, D)),
        "rwkv_ln_b": nrm((NR, D), 0.02),
        "rwkv_w_o": nrm((NR, D, D), D ** -0.5),
        "attn_w_qkv": nrm((NA, D, qkv_cols), D ** -0.5),
        "attn_q_gain": gain((NA, N_GROUPS, ATTN_HEAD_DIM)),
        "attn_k_gain": gain((NA, N_GROUPS, ATTN_HEAD_DIM)),
        "attn_w_o": nrm((NA, ATTN_HEADS * ATTN_HEAD_DIM, D), (ATTN_HEADS * ATTN_HEAD_DIM) ** -0.5),
        "rel_bias": nrm((NUM_BUCKETS, N_GROUPS * ATTN_HEADS), 0.5),
    }


def reference(x, mix_norm, ffn_norm, ffn_w_in, ffn_w_out, rwkv_mu, rwkv_w_rkv, rwkv_w0,
              rwkv_w1, rwkv_w2, rwkv_a0, rwkv_a1, rwkv_a2, rwkv_v0, rwkv_v1, rwkv_v2,
              rwkv_g1, rwkv_g2, rwkv_k_k, rwkv_k_a, rwkv_r_k, rwkv_ln_w, rwkv_ln_b, rwkv_w_o,
              attn_w_qkv, attn_q_gain, attn_k_gain, attn_w_o, rel_bias):
    v_first = None
    for layer in range(DEPTH):
        h = rms_norm(x, mix_norm[layer])
        idx = layer // N_MIXERS
        if layer % N_MIXERS == 0:
            v_gate = None if idx == 0 else (rwkv_v0[idx - 1], rwkv_v1[idx - 1], rwkv_v2[idx - 1])
            y, v_first = rwkv7_time_mix(
                h, v_first, rwkv_mu[idx], rwkv_w_rkv[idx], rwkv_w0[idx], rwkv_w1[idx], rwkv_w2[idx],
                rwkv_a0[idx], rwkv_a1[idx], rwkv_a2[idx], rwkv_g1[idx], rwkv_g2[idx],
                rwkv_k_k[idx], rwkv_k_a[idx], rwkv_r_k[idx], rwkv_ln_w[idx], rwkv_ln_b[idx],
                rwkv_w_o[idx], v_gate)
        else:
            y = dilated_attention_mixer(h, attn_w_qkv[idx], attn_q_gain[idx], attn_k_gain[idx],
                                        attn_w_o[idx], rel_bias)
        x = x + y
        x = x + swiglu_ffn(rms_norm(x, ffn_norm[layer]), ffn_w_in[layer], ffn_w_out[layer])
    return x
```

```python
import functools
import math

import numpy as np
import jax
import jax.numpy as jnp
from jax import lax
from jax.experimental import pallas as pl
from jax.experimental.pallas import tpu as pltpu

F32 = jnp.float32
BF16 = jnp.bfloat16

HEAD = 64
RMS_EPS = 1e-6
RWKV_LN_EPS = 64e-5
DILATION_PAIRS = ((128, 1), (512, 4), (2048, 16))
BAND = 128
NUM_BUCKETS = 32
MAX_DISTANCE = 2048
NEG_INF = -1e30
CHUNK = 64
HEADSUM_W = 256
LSE_W = 128
VMEM_LIMIT = 56 * 1024 * 1024


def _cparams(sem):
    return pltpu.CompilerParams(dimension_semantics=sem, vmem_limit_bytes=VMEM_LIMIT)


def _vmem_spec():
    return pl.BlockSpec(memory_space=pltpu.VMEM)


def _rms(x, gain):
    return x * lax.rsqrt(jnp.mean(x * x, axis=-1, keepdims=True) + RMS_EPS) * gain


def _sigmoid(x):
    return 1.0 / (1.0 + jnp.exp(-x))


def _dot(a, b):
    return jnp.dot(a.astype(BF16), b.astype(BF16), preferred_element_type=F32)


def _dot_nt(a, b):
    return lax.dot_general(a.astype(BF16), b.astype(BF16), (((1,), (1,)), ((), ())),
                           preferred_element_type=F32)


def _dot_tn(a, b):
    return lax.dot_general(a.astype(BF16), b.astype(BF16), (((0,), (0,)), ((), ())),
                           preferred_element_type=F32)


def _headsum(x, bd_ref):
    d = x.shape[-1]
    bd = bd_ref[...]
    parts = [_dot(x[:, c:c + HEADSUM_W], bd) for c in range(0, d, HEADSUM_W)]
    return jnp.concatenate(parts, axis=-1)


def _block_diag_ones():
    i = np.arange(HEADSUM_W) // HEAD
    return jnp.asarray((i[:, None] == i[None, :]).astype(np.float32), dtype=BF16)


def _ffn_kernel(x_ref, g_ref, win_ref, wout_ref, o_ref, *, d_ff, tf):
    x = x_ref[0]
    h = _rms(x, g_ref[...]).astype(BF16)
    acc = x
    for c in range(0, d_ff, tf):
        gate = jnp.dot(h, win_ref[:, c:c + tf], preferred_element_type=F32)
        up = jnp.dot(h, win_ref[:, d_ff + c:d_ff + c + tf], preferred_element_type=F32)
        act = (gate * _sigmoid(gate) * up).astype(BF16)
        acc = acc + jnp.dot(act, wout_ref[c:c + tf, :], preferred_element_type=F32)
    o_ref[0] = acc


def _ffn(x, gain, w_in, w_out, *, tm=512, tf=256):
    B, S, D = x.shape
    d_ff = w_out.shape[0]
    return pl.pallas_call(
        functools.partial(_ffn_kernel, d_ff=d_ff, tf=tf),
        grid=(B, S // tm),
        in_specs=[pl.BlockSpec((1, tm, D), lambda b, i: (b, i, 0)),
                  pl.BlockSpec((1, D), lambda b, i: (0, 0)),
                  _vmem_spec(), _vmem_spec()],
        out_specs=pl.BlockSpec((1, tm, D), lambda b, i: (b, i, 0)),
        out_shape=jax.ShapeDtypeStruct((B, S, D), F32),
        compiler_params=_cparams(("parallel", "parallel")),
        name="ffn",
    )(x, gain.reshape(1, D), w_in.astype(BF16), w_out.astype(BF16))


def _rwkv_proj_kernel(*refs, has_vgate):
    if has_vgate:
        (x_ref, xp_ref, gain_ref, mu_ref, wrkv_ref, w0_ref, w1_ref, w2_ref, a0_ref, a1_ref, a2_ref,
         g1_ref, g2_ref, kk_ref, ka_ref, bd_ref, v0_ref, v1_ref, v2_ref, vf_ref,
         r_o, k_o, v_o, lw_o, kn_o, as_o, g_o) = refs
    else:
        (x_ref, xp_ref, gain_ref, mu_ref, wrkv_ref, w0_ref, w1_ref, w2_ref, a0_ref, a1_ref, a2_ref,
         g1_ref, g2_ref, kk_ref, ka_ref, bd_ref,
         r_o, k_o, v_o, lw_o, kn_o, as_o, g_o) = refs
    gain = gain_ref[...]
    h = _rms(x_ref[0], gain)
    tm = h.shape[0]
    hp_row = _rms(xp_ref[0], gain)[7:8, :]
    hp_row = jnp.where(pl.program_id(1) == 0, 0.0, hp_row)
    row = lax.broadcasted_iota(jnp.int32, h.shape, 0)
    h_prev = jnp.where(row == 0, hp_row, pltpu.roll(h, 1, axis=0))
    xx = h_prev - h

    def mix(c):
        return (h + xx * mu_ref[c:c + 1, :]).astype(BF16)

    xr, xk, xv, xw, xa, xg = (mix(c) for c in range(6))
    r = jnp.dot(xr, wrkv_ref[0], preferred_element_type=F32)
    k = jnp.dot(xk, wrkv_ref[1], preferred_element_type=F32)
    v = jnp.dot(xv, wrkv_ref[2], preferred_element_type=F32)

    wl = w0_ref[...] + _dot(jnp.tanh(jnp.dot(xw, w1_ref[...], preferred_element_type=F32)), w2_ref[...])
    w = -(jnp.maximum(-wl, 0.0) + jnp.log(1.0 + jnp.exp(-jnp.abs(wl)))) - 0.5
    lw_o[0] = -jnp.exp(w)
    a = _sigmoid(a0_ref[...] + _dot(jnp.dot(xa, a1_ref[...], preferred_element_type=F32), a2_ref[...]))
    g = _dot(_sigmoid(jnp.dot(xg, g1_ref[...], preferred_element_type=F32)), g2_ref[...])
    if has_vgate:
        gate = _sigmoid(v0_ref[...] + _dot(jnp.dot(xv, v1_ref[...], preferred_element_type=F32), v2_ref[...]))
        v = v + (vf_ref[0].astype(F32) - v) * gate

    kk = k * kk_ref[...]
    kn = kk * lax.rsqrt(jnp.maximum(_headsum(kk * kk, bd_ref), 1e-24))
    k = k * (1.0 + (a - 1.0) * ka_ref[...])
    r_o[0] = r.astype(BF16)
    k_o[0] = k.astype(BF16)
    v_o[0] = v.astype(BF16)
    kn_o[0] = kn.astype(BF16)
    as_o[0] = a.astype(BF16)
    g_o[0] = g.astype(BF16)


def _rwkv_proj(x, gain, mu, w_rkv, w0, w1, w2, a0, a1, a2, g1, g2, k_k, k_a, v_gate, v_first, *, tm=512):
    B, S, D = x.shape
    has_vgate = v_gate is not None
    row = lambda t: t.reshape(1, D)
    tile = pl.BlockSpec((1, tm, D), lambda b, i: (b, i, 0))
    prev = pl.BlockSpec((1, 8, D), lambda b, i: (b, jnp.maximum(i * (tm // 8) - 1, 0), 0))
    vec = pl.BlockSpec((1, D), lambda b, i: (0, 0))
    args = [x, x, row(gain), mu, w_rkv.astype(BF16), row(w0), w1.astype(BF16), w2.astype(BF16),
            row(a0), a1.astype(BF16), a2.astype(BF16), g1.astype(BF16), g2.astype(BF16),
            row(k_k), row(k_a), _block_diag_ones()]
    specs = [tile, prev, vec, _vmem_spec(), _vmem_spec(), vec, _vmem_spec(), _vmem_spec(),
             vec, _vmem_spec(), _vmem_spec(), _vmem_spec(), _vmem_spec(), vec, vec, _vmem_spec()]
    if has_vgate:
        v0, v1, v2 = v_gate
        args += [row(v0), v1.astype(BF16), v2.astype(BF16), v_first]
        specs += [vec, _vmem_spec(), _vmem_spec(), tile]
    bshape = jax.ShapeDtypeStruct((B, S, D), BF16)
    out_shape = [bshape, bshape, bshape, jax.ShapeDtypeStruct((B, S, D), F32), bshape, bshape, bshape]
    return pl.pallas_call(
        functools.partial(_rwkv_proj_kernel, has_vgate=has_vgate),
        grid=(B, S // tm),
        in_specs=specs,
        out_specs=[tile] * 7,
        out_shape=out_shape,
        compiler_params=_cparams(("parallel", "parallel")),
        name="rwkv_proj",
    )(*args)


def _rwkv_scan_kernel(r_ref, k_ref, v_ref, lw_ref, kn_ref, as_ref, tri_ref, y_ref, state_ref, *, n_heads):
    C = CHUNK

    @pl.when(pl.program_id(1) == 0)
    def _():
        state_ref[...] = jnp.zeros_like(state_ref)

    lw = lw_ref[0]
    cum = jnp.dot(tri_ref[...], lw, preferred_element_type=F32, precision=lax.Precision.HIGHEST)
    tot = cum[C - 1:C, :]
    r = r_ref[0].astype(F32)
    k = k_ref[0].astype(F32)
    kn = kn_ref[0].astype(F32)
    b = kn * as_ref[0].astype(F32)
    w_inv = jnp.exp(-cum)
    w_end = jnp.exp(tot - cum)
    r_d = (r * jnp.exp(cum)).astype(BF16)
    a_d = (-kn * jnp.exp(cum - lw)).astype(BF16)
    k_d = (k * w_inv).astype(BF16)
    b_d = (b * w_inv).astype(BF16)
    k_e = (k * w_end).astype(BF16)
    b_e = (b * w_end).astype(BF16)
    w_tot = jnp.exp(tot)
    v = v_ref[0]

    ri = lax.broadcasted_iota(jnp.int32, (C, C), 0)
    ci = lax.broadcasted_iota(jnp.int32, (C, C), 1)
    strict = ri > ci
    incl = ri >= ci
    eye = jnp.where(ri == ci, 1.0, 0.0).astype(F32)

    for hd in range(n_heads):
        sl = slice(hd * HEAD, (hd + 1) * HEAD)
        rh, ah, kh, bh, vh = r_d[:, sl], a_d[:, sl], k_d[:, sl], b_d[:, sl], v[:, sl]
        a_ak = jnp.where(strict, _dot_nt(ah, kh), 0.0)
        a_ab = jnp.where(strict, _dot_nt(ah, bh), 0.0)
        a_rk = jnp.where(incl, _dot_nt(rh, kh), 0.0)
        a_rb = jnp.where(incl, _dot_nt(rh, bh), 0.0)
        p = eye + a_ab
        q = a_ab
        for _ in range(int(math.log2(C)) - 1):
            q = _dot(q, q)
            p = p + _dot(p, q)
        s0 = state_ref[hd]
        u = _dot(p, _dot_nt(ah, s0) + _dot(a_ak, vh))
        y = _dot_nt(rh, s0) + _dot(a_rk, vh) + _dot(a_rb, u)
        y_ref[0, :, sl] = y.astype(y_ref.dtype)
        state_ref[hd] = s0 * w_tot[:, sl] + _dot_tn(vh, k_e[:, sl]) + _dot_tn(u, b_e[:, sl])


def _rwkv_scan(r, k, v, lw, kn, a_sig):
    B, S, D = r.shape
    n_heads = D // HEAD
    tile = pl.BlockSpec((1, CHUNK, D), lambda b, c: (b, c, 0))
    tri = jnp.asarray(np.tril(np.ones((CHUNK, CHUNK), np.float32)))
    return pl.pallas_call(
        functools.partial(_rwkv_scan_kernel, n_heads=n_heads),
        grid=(B, S // CHUNK),
        in_specs=[tile] * 6 + [pl.BlockSpec((CHUNK, CHUNK), lambda b, c: (0, 0))],
        out_specs=tile,
        out_shape=jax.ShapeDtypeStruct((B, S, D), BF16),
        scratch_shapes=[pltpu.VMEM((n_heads, HEAD, HEAD), F32)],
        compiler_params=_cparams(("parallel", "arbitrary")),
        name="rwkv_scan",
    )(r, k, v, lw, kn, a_sig, tri)


def _rwkv_out_kernel(x_ref, y_ref, r_ref, k_ref, v_ref, g_ref, lnw_ref, lnb_ref, rk_ref, bd_ref, wo_ref, o_ref):
    y = y_ref[0].astype(F32)
    inv_n = 1.0 / HEAD
    mean = _headsum(y, bd_ref) * inv_n
    d = y - mean
    var = _headsum(d * d, bd_ref) * inv_n
    yn = d * lax.rsqrt(var + RWKV_LN_EPS) * lnw_ref[...] + lnb_ref[...]
    rk = r_ref[0].astype(F32) * k_ref[0].astype(F32) * rk_ref[...]
    bonus = _headsum(rk, bd_ref) * v_ref[0].astype(F32)
    z = ((yn + bonus) * g_ref[0].astype(F32)).astype(BF16)
    o_ref[0] = x_ref[0] + jnp.dot(z, wo_ref[...], preferred_element_type=F32)


def _rwkv_out(x, y, r, k, v, g, ln_w, ln_b, r_k, w_o, *, tm=512):
    B, S, D = x.shape
    tile = pl.BlockSpec((1, tm, D), lambda b, i: (b, i, 0))
    vec = pl.BlockSpec((1, D), lambda b, i: (0, 0))
    return pl.pallas_call(
        _rwkv_out_kernel,
        grid=(B, S // tm),
        in_specs=[tile] * 6 + [vec, vec, vec, _vmem_spec(), _vmem_spec()],
        out_specs=tile,
        out_shape=jax.ShapeDtypeStruct((B, S, D), F32),
        compiler_params=_cparams(("parallel", "parallel")),
        name="rwkv_out",
    )(x, y, r, k, v, g, ln_w.reshape(1, D), ln_b.reshape(1, D), r_k.reshape(1, D),
      _block_diag_ones(), w_o.astype(BF16))


def _attn_qkv_kernel(x_ref, gain_ref, w_ref, qg_ref, kg_ref, bd_ref, q_o, k_o, v_o, *, dm):
    h = _rms(x_ref[0], gain_ref[...]).astype(BF16)
    qkv = jnp.dot(h, w_ref[...], preferred_element_type=F32)
    q, k, v = qkv[:, :dm], qkv[:, dm:2 * dm], qkv[:, 2 * dm:]
    inv_n = 1.0 / HEAD
    q = q * lax.rsqrt(_headsum(q * q, bd_ref) * inv_n + RMS_EPS) * qg_ref[...]
    k = k * lax.rsqrt(_headsum(k * k, bd_ref) * inv_n + RMS_EPS) * kg_ref[...]
    q_o[0, 0] = q.astype(BF16)
    k_o[0, 0] = k.astype(BF16)
    v_o[0, 0] = v.astype(BF16)


def _attn_qkv(x, gain, w, q_gain, k_gain, dilation):
    B, S, D = x.shape
    dm = w.shape[1] // 3
    sd = S // dilation
    tm = min(512, sd)
    n_heads = dm // HEAD
    xv = x.reshape(B, sd, dilation * D)
    scale = HEAD ** -0.5
    out = jax.ShapeDtypeStruct((B, dilation, sd, dm), BF16)
    ospec = pl.BlockSpec((1, 1, tm, dm), lambda b, r, j: (b, r, j, 0))
    vec = lambda n: pl.BlockSpec((1, n), lambda b, r, j: (0, 0))
    return pl.pallas_call(
        functools.partial(_attn_qkv_kernel, dm=dm),
        grid=(B, dilation, sd // tm),
        in_specs=[pl.BlockSpec((1, tm, D), lambda b, r, j: (b, j, r)), vec(D), _vmem_spec(),
                  vec(dm), vec(dm), _vmem_spec()],
        out_specs=[ospec] * 3,
        out_shape=[out] * 3,
        compiler_params=_cparams(("parallel", "parallel", "parallel")),
        name=f"attn_qkv_d{dilation}",
    )(xv, gain.reshape(1, D), w.astype(BF16), (jnp.tile(q_gain, n_heads) * scale).reshape(1, dm),
      jnp.tile(k_gain, n_heads).reshape(1, dm), _block_diag_ones())


def _attn_core_kernel(q_ref, kp_ref, kc_ref, vp_ref, vc_ref, bias_ref, o_ref, lse_ref, *, n_heads):
    q = q_ref[0, 0]
    kp, kc, vp, vc = kp_ref[0, 0], kc_ref[0, 0], vp_ref[0, 0], vc_ref[0, 0]
    col = lax.broadcasted_iota(jnp.int32, (BAND, 2 * BAND), 1)
    key_ok = (col >= BAND) | (pl.program_id(2) > 0)
    lane = lax.broadcasted_iota(jnp.int32, (BAND, LSE_W), 1)
    lse_pad = jnp.zeros((BAND, LSE_W), F32)
    for hd in range(n_heads):
        sl = slice(hd * HEAD, (hd + 1) * HEAD)
        kh = jnp.concatenate([kp[:, sl], kc[:, sl]], axis=0)
        vh = jnp.concatenate([vp[:, sl], vc[:, sl]], axis=0)
        logits = lax.dot_general(q[:, sl], kh, (((1,), (1,)), ((), ())), preferred_element_type=F32)
        logits = jnp.where(key_ok, logits + bias_ref[hd], NEG_INF)
        m = jnp.max(logits, axis=-1, keepdims=True)
        p = jnp.exp(logits - m)
        l = jnp.sum(p, axis=-1, keepdims=True)
        o = jnp.dot(p.astype(BF16), vh, preferred_element_type=F32) / l
        o_ref[0, :, sl] = o.astype(o_ref.dtype)
        lse_pad = jnp.where(lane == hd, m + jnp.log(l), lse_pad)
    lse_ref[0] = lse_pad


def _attn_core(q, k, v, bias, dilation):
    B, d, sd, dm = q.shape
    n_heads = dm // HEAD
    nb = sd // BAND
    cur = pl.BlockSpec((1, 1, BAND, dm), lambda b, r, n: (b, r, n, 0))
    prv = pl.BlockSpec((1, 1, BAND, dm), lambda b, r, n: (b, r, jnp.maximum(n - 1, 0), 0))
    o, lse = pl.pallas_call(
        functools.partial(_attn_core_kernel, n_heads=n_heads),
        grid=(B, d, nb),
        in_specs=[cur, prv, cur, prv, cur, _vmem_spec()],
        out_specs=[pl.BlockSpec((1, BAND, dm), lambda b, r, n: (b, n, r)),
                   pl.BlockSpec((1, BAND, LSE_W), lambda b, r, n: (b, n, r))],
        out_shape=[jax.ShapeDtypeStruct((B, sd, d * dm), BF16),
                   jax.ShapeDtypeStruct((B, sd, d * LSE_W), F32)],
        compiler_params=_cparams(("parallel", "parallel", "parallel")),
        name=f"attn_core_d{dilation}",
    )(q, k, k, v, v, bias)
    return o.reshape(B, sd * d, dm), lse.reshape(B, sd * d, LSE_W)


def _attn_out_kernel(x_ref, o1_ref, o2_ref, o3_ref, l1_ref, l2_ref, l3_ref, ex_ref, wo_ref, out_ref):
    ls = [l1_ref[0], l2_ref[0], l3_ref[0]]
    os_ = [o1_ref[0], o2_ref[0], o3_ref[0]]
    m = jnp.maximum(jnp.maximum(ls[0], ls[1]), ls[2])
    es = [jnp.exp(l - m) for l in ls]
    inv = 1.0 / (es[0] + es[1] + es[2])
    acc = None
    for e, o in zip(es, os_):
        wgt = e * inv
        hi = wgt.astype(BF16)
        lo = (wgt - hi.astype(F32)).astype(BF16)
        wfull = (jnp.dot(hi, ex_ref[...], preferred_element_type=F32)
                 + jnp.dot(lo, ex_ref[...], preferred_element_type=F32))
        term = wfull * o.astype(F32)
        acc = term if acc is None else acc + term
    out_ref[0] = x_ref[0] + jnp.dot(acc.astype(BF16), wo_ref[...], preferred_element_type=F32)


def _attn_out(x, outs, lses, w_o, *, tm=512):
    B, S, D = x.shape
    dm = w_o.shape[0]
    n_heads = dm // HEAD
    expand = np.zeros((LSE_W, dm), np.float32)
    for hd in range(n_heads):
        expand[hd, hd * HEAD:(hd + 1) * HEAD] = 1.0
    tile = lambda w: pl.BlockSpec((1, tm, w), lambda b, i: (b, i, 0))
    return pl.pallas_call(
        _attn_out_kernel,
        grid=(B, S // tm),
        in_specs=[tile(D)] + [tile(dm)] * 3 + [tile(LSE_W)] * 3 + [_vmem_spec(), _vmem_spec()],
        out_specs=tile(D),
        out_shape=jax.ShapeDtypeStruct((B, S, D), F32),
        compiler_params=_cparams(("parallel", "parallel")),
        name="attn_out",
    )(x, *outs, *lses, jnp.asarray(expand, dtype=BF16), w_o.astype(BF16))


def _t5_bucket(dist):
    max_exact = NUM_BUCKETS // 2
    large = max_exact + (jnp.log(jnp.maximum(dist, 1).astype(F32) / max_exact)
                         / math.log(MAX_DISTANCE / max_exact) * (NUM_BUCKETS - max_exact)).astype(jnp.int32)
    large = jnp.minimum(large, NUM_BUCKETS - 1)
    return jnp.where(dist < max_exact, dist, large)


def _band_bias(table_g, window, dilation):
    i = jnp.arange(BAND)[:, None]
    c = jnp.arange(2 * BAND)[None, :]
    steps = i + BAND - c
    allowed = (steps >= 0) & (steps <= window // dilation)
    bucket = _t5_bucket(jnp.maximum(steps, 0) * dilation)
    bias = jnp.take(table_g, bucket, axis=0).astype(F32)
    bias = jnp.where(allowed[..., None], bias, NEG_INF)
    return jnp.transpose(bias, (2, 0, 1))


def _attn_mixer(x, gain, w_qkv, q_gain, k_gain, w_o, rel_bias):
    dm = w_o.shape[0]
    n_heads = dm // HEAD
    outs, lses = [], []
    for g, (window, dilation) in enumerate(DILATION_PAIRS):
        wg = w_qkv[:, g * 3 * dm:(g + 1) * 3 * dm]
        q, k, v = _attn_qkv(x, gain, wg, q_gain[g], k_gain[g], dilation)
        bias = _band_bias(rel_bias[:, g * n_heads:(g + 1) * n_heads], window, dilation)
        o, lse = _attn_core(q, k, v, bias, dilation)
        outs.append(o)
        lses.append(lse)
    return _attn_out(x, outs, lses, w_o)


def kernel(x, mix_norm, ffn_norm, ffn_w_in, ffn_w_out, rwkv_mu, rwkv_w_rkv, rwkv_w0, rwkv_w1, rwkv_w2, rwkv_a0, rwkv_a1, rwkv_a2, rwkv_v0, rwkv_v1, rwkv_v2, rwkv_g1, rwkv_g2, rwkv_k_k, rwkv_k_a, rwkv_r_k, rwkv_ln_w, rwkv_ln_b, rwkv_w_o, attn_w_qkv, attn_q_gain, attn_k_gain, attn_w_o, rel_bias):
    depth = mix_norm.shape[0]
    v_first = None
    for layer in range(depth):
        i = layer // 2
        if layer % 2 == 0:
            v_gate = None if i == 0 else (rwkv_v0[i - 1], rwkv_v1[i - 1], rwkv_v2[i - 1])
            r, k, v, lw, kn, a_sig, g = _rwkv_proj(
                x, mix_norm[layer], rwkv_mu[i], rwkv_w_rkv[i], rwkv_w0[i], rwkv_w1[i], rwkv_w2[i],
                rwkv_a0[i], rwkv_a1[i], rwkv_a2[i], rwkv_g1[i], rwkv_g2[i], rwkv_k_k[i], rwkv_k_a[i],
                v_gate, v_first)
            if v_first is None:
                v_first = v
            y = _rwkv_scan(r, k, v, lw, kn, a_sig)
            x = _rwkv_out(x, y, r, k, v, g, rwkv_ln_w[i], rwkv_ln_b[i], rwkv_r_k[i], rwkv_w_o[i])
        else:
            x = _attn_mixer(x, mix_norm[layer], attn_w_qkv[i], attn_q_gain[i], attn_k_gain[i],
                            attn_w_o[i], rel_bias)
        x = _ffn(x, ffn_norm[layer], ffn_w_in[layer], ffn_w_out[layer])
    return x
```

```python
import functools
import math

import numpy as np
import jax
import jax.numpy as jnp
from jax import lax
from jax.experimental import pallas as pl
from jax.experimental.pallas import tpu as pltpu

F32 = jnp.float32
BF16 = jnp.bfloat16

HEAD = 64
RMS_EPS = 1e-6
RWKV_LN_EPS = 64e-5
DILATION_PAIRS = ((128, 1), (512, 4), (2048, 16))
BAND = 128
NUM_BUCKETS = 32
MAX_DISTANCE = 2048
NEG_INF = -1e30
CHUNK = 64
HEADSUM_W = 256
LSE_W = 128
VMEM_LIMIT = 56 * 1024 * 1024


def _cparams(sem):
    return pltpu.CompilerParams(dimension_semantics=sem, vmem_limit_bytes=VMEM_LIMIT)


def _vmem_spec():
    return pl.BlockSpec(memory_space=pltpu.VMEM)


def _rms(x, gain):
    return x * lax.rsqrt(jnp.mean(x * x, axis=-1, keepdims=True) + RMS_EPS) * gain


def _sigmoid(x):
    return 1.0 / (1.0 + jnp.exp(-x))


def _dot(a, b):
    return jnp.dot(a.astype(BF16), b.astype(BF16), preferred_element_type=F32)


def _dot_nt(a, b):
    return lax.dot_general(a.astype(BF16), b.astype(BF16), (((1,), (1,)), ((), ())),
                           preferred_element_type=F32)


def _dot_tn(a, b):
    return lax.dot_general(a.astype(BF16), b.astype(BF16), (((0,), (0,)), ((), ())),
                           preferred_element_type=F32)


def _headsum(x, bd_ref):
    d = x.shape[-1]
    bd = bd_ref[...]
    parts = [_dot(x[:, c:c + HEADSUM_W], bd) for c in range(0, d, HEADSUM_W)]
    return jnp.concatenate(parts, axis=-1)


def _block_diag_ones():
    i = np.arange(HEADSUM_W) // HEAD
    return jnp.asarray((i[:, None] == i[None, :]).astype(np.float32), dtype=BF16)


def _ffn_kernel(x_ref, g_ref, win_ref, wout_ref, o_ref, *, d_ff, tf):
    x = x_ref[0]
    h = _rms(x, g_ref[...]).astype(BF16)
    acc = x
    for c in range(0, d_ff, tf):
        gate = jnp.dot(h, win_ref[:, c:c + tf], preferred_element_type=F32)
        up = jnp.dot(h, win_ref[:, d_ff + c:d_ff + c + tf], preferred_element_type=F32)
        act = (gate * _sigmoid(gate) * up).astype(BF16)
        acc = acc + jnp.dot(act, wout_ref[c:c + tf, :], preferred_element_type=F32)
    o_ref[0] = acc


def _ffn(x, gain, w_in, w_out, *, tm=512, tf=256):
    B, S, D = x.shape
    d_ff = w_out.shape[0]
    return pl.pallas_call(
        functools.partial(_ffn_kernel, d_ff=d_ff, tf=tf),
        grid=(B, S // tm),
        in_specs=[pl.BlockSpec((1, tm, D), lambda b, i: (b, i, 0)),
                  pl.BlockSpec((1, D), lambda b, i: (0, 0)),
                  _vmem_spec(), _vmem_spec()],
        out_specs=pl.BlockSpec((1, tm, D), lambda b, i: (b, i, 0)),
        out_shape=jax.ShapeDtypeStruct((B, S, D), F32),
        compiler_params=_cparams(("parallel", "parallel")),
        name="ffn",
    )(x, gain.reshape(1, D), w_in.astype(BF16), w_out.astype(BF16))


def _rwkv_proj_kernel(*refs, has_vgate):
    if has_vgate:
        (x_ref, xp_ref, gain_ref, mu_ref, wrkv_ref, w0_ref, w1_ref, w2_ref, a0_ref, a1_ref, a2_ref,
         g1_ref, g2_ref, kk_ref, ka_ref, bd_ref, v0_ref, v1_ref, v2_ref, vf_ref,
         r_o, k_o, v_o, lw_o, kn_o, as_o, g_o) = refs
    else:
        (x_ref, xp_ref, gain_ref, mu_ref, wrkv_ref, w0_ref, w1_ref, w2_ref, a0_ref, a1_ref, a2_ref,
         g1_ref, g2_ref, kk_ref, ka_ref, bd_ref,
         r_o, k_o, v_o, lw_o, kn_o, as_o, g_o) = refs
    gain = gain_ref[...]
    h = _rms(x_ref[0], gain)
    tm = h.shape[0]
    hp_row = _rms(xp_ref[0], gain)[7:8, :]
    hp_row = jnp.where(pl.program_id(1) == 0, 0.0, hp_row)
    row = lax.broadcasted_iota(jnp.int32, h.shape, 0)
    h_prev = jnp.where(row == 0, hp_row, pltpu.roll(h, 1, axis=0))
    xx = h_prev - h

    def mix(c):
        return (h + xx * mu_ref[c:c + 1, :]).astype(BF16)

    xr, xk, xv, xw, xa, xg = (mix(c) for c in range(6))
    r = jnp.dot(xr, wrkv_ref[0], preferred_element_type=F32)
    k = jnp.dot(xk, wrkv_ref[1], preferred_element_type=F32)
    v = jnp.dot(xv, wrkv_ref[2], preferred_element_type=F32)

    wl = w0_ref[...] + _dot(jnp.tanh(jnp.dot(xw, w1_ref[...], preferred_element_type=F32)), w2_ref[...])
    w = -(jnp.maximum(-wl, 0.0) + jnp.log(1.0 + jnp.exp(-jnp.abs(wl)))) - 0.5
    lw_o[0] = -jnp.exp(w)
    a = _sigmoid(a0_ref[...] + _dot(jnp.dot(xa, a1_ref[...], preferred_element_type=F32), a2_ref[...]))
    g = _dot(_sigmoid(jnp.dot(xg, g1_ref[...], preferred_element_type=F32)), g2_ref[...])
    if has_vgate:
        gate = _sigmoid(v0_ref[...] + _dot(jnp.dot(xv, v1_ref[...], preferred_element_type=F32), v2_ref[...]))
        v = v + (vf_ref[0].astype(F32) - v) * gate

    kk = k * kk_ref[...]
    kn = kk * lax.rsqrt(jnp.maximum(_headsum(kk * kk, bd_ref), 1e-24))
    k = k * (1.0 + (a - 1.0) * ka_ref[...])
    r_o[0] = r.astype(BF16)
    k_o[0] = k.astype(BF16)
    v_o[0] = v.astype(BF16)
    kn_o[0] = kn.astype(BF16)
    as_o[0] = a.astype(BF16)
    g_o[0] = g.astype(BF16)


def _rwkv_proj(x, gain, mu, w_rkv, w0, w1, w2, a0, a1, a2, g1, g2, k_k, k_a, v_gate, v_first, *, tm=512):
    B, S, D = x.shape
    has_vgate = v_gate is not None
    row = lambda t: t.reshape(1, D)
    tile = pl.BlockSpec((1, tm, D), lambda b, i: (b, i, 0))
    prev = pl.BlockSpec((1, 8, D), lambda b, i: (b, jnp.maximum(i * (tm // 8) - 1, 0), 0))
    vec = pl.BlockSpec((1, D), lambda b, i: (0, 0))
    args = [x, x, row(gain), mu, w_rkv.astype(BF16), row(w0), w1.astype(BF16), w2.astype(BF16),
            row(a0), a1.astype(BF16), a2.astype(BF16), g1.astype(BF16), g2.astype(BF16),
            row(k_k), row(k_a), _block_diag_ones()]
    specs = [tile, prev, vec, _vmem_spec(), _vmem_spec(), vec, _vmem_spec(), _vmem_spec(),
             vec, _vmem_spec(), _vmem_spec(), _vmem_spec(), _vmem_spec(), vec, vec, _vmem_spec()]
    if has_vgate:
        v0, v1, v2 = v_gate
        args += [row(v0), v1.astype(BF16), v2.astype(BF16), v_first]
        specs += [vec, _vmem_spec(), _vmem_spec(), tile]
    bshape = jax.ShapeDtypeStruct((B, S, D), BF16)
    out_shape = [bshape, bshape, bshape, jax.ShapeDtypeStruct((B, S, D), F32), bshape, bshape, bshape]
    return pl.pallas_call(
        functools.partial(_rwkv_proj_kernel, has_vgate=has_vgate),
        grid=(B, S // tm),
        in_specs=specs,
        out_specs=[tile] * 7,
        out_shape=out_shape,
        compiler_params=_cparams(("parallel", "parallel")),
        name="rwkv_proj",
    )(*args)


def _rwkv_scan_kernel(r_ref, k_ref, v_ref, lw_ref, kn_ref, as_ref, tri_ref, y_ref, state_ref, *, n_heads):
    C = CHUNK

    @pl.when(pl.program_id(1) == 0)
    def _():
        state_ref[...] = jnp.zeros_like(state_ref)

    lw = lw_ref[0]
    cum = jnp.dot(tri_ref[...], lw, preferred_element_type=F32, precision=lax.Precision.HIGHEST)
    tot = cum[C - 1:C, :]
    r = r_ref[0].astype(F32)
    k = k_ref[0].astype(F32)
    kn = kn_ref[0].astype(F32)
    b = kn * as_ref[0].astype(F32)
    w_inv = jnp.exp(-cum)
    w_end = jnp.exp(tot - cum)
    r_d = (r * jnp.exp(cum)).astype(BF16)
    a_d = (-kn * jnp.exp(cum - lw)).astype(BF16)
    k_d = (k * w_inv).astype(BF16)
    b_d = (b * w_inv).astype(BF16)
    k_e = (k * w_end).astype(BF16)
    b_e = (b * w_end).astype(BF16)
    w_tot = jnp.exp(tot)
    v = v_ref[0]

    ri = lax.broadcasted_iota(jnp.int32, (C, C), 0)
    ci = lax.broadcasted_iota(jnp.int32, (C, C), 1)
    strict = ri > ci
    incl = ri >= ci
    eye = jnp.where(ri == ci, 1.0, 0.0).astype(F32)

    heads = range(n_heads)
    sls = [slice(hd * HEAD, (hd + 1) * HEAD) for hd in heads]
    rh = [r_d[:, sl] for sl in sls]
    ah = [a_d[:, sl] for sl in sls]
    kh = [k_d[:, sl] for sl in sls]
    bh = [b_d[:, sl] for sl in sls]
    vh = [v[:, sl] for sl in sls]
    a_ab = [jnp.where(strict, _dot_nt(ah[h], bh[h]), 0.0) for h in heads]
    a_ak = [jnp.where(strict, _dot_nt(ah[h], kh[h]), 0.0).astype(BF16) for h in heads]
    a_rk = [jnp.where(incl, _dot_nt(rh[h], kh[h]), 0.0).astype(BF16) for h in heads]
    a_rb = [jnp.where(incl, _dot_nt(rh[h], bh[h]), 0.0).astype(BF16) for h in heads]
    p = [eye + a_ab[h] for h in heads]
    q = a_ab
    for _ in range(int(math.log2(C)) - 1):
        q = [_dot(q[h], q[h]) for h in heads]
        p = [p[h] + _dot(p[h], q[h]) for h in heads]
    s0 = [state_ref[h] for h in heads]
    rhs = [_dot_nt(ah[h], s0[h]) + _dot(a_ak[h], vh[h]) for h in heads]
    u = [_dot(p[h], rhs[h]) for h in heads]
    y = [_dot_nt(rh[h], s0[h]) + _dot(a_rk[h], vh[h]) + _dot(a_rb[h], u[h]) for h in heads]
    for h in heads:
        y_ref[0, :, sls[h]] = y[h].astype(y_ref.dtype)
        state_ref[h] = (s0[h] * w_tot[:, sls[h]] + _dot_tn(vh[h], k_e[:, sls[h]])
                        + _dot_tn(u[h], b_e[:, sls[h]]))


def _rwkv_scan(r, k, v, lw, kn, a_sig):
    B, S, D = r.shape
    n_heads = D // HEAD
    tile = pl.BlockSpec((1, CHUNK, D), lambda b, c: (b, c, 0))
    tri = jnp.asarray(np.tril(np.ones((CHUNK, CHUNK), np.float32)))
    return pl.pallas_call(
        functools.partial(_rwkv_scan_kernel, n_heads=n_heads),
        grid=(B, S // CHUNK),
        in_specs=[tile] * 6 + [pl.BlockSpec((CHUNK, CHUNK), lambda b, c: (0, 0))],
        out_specs=tile,
        out_shape=jax.ShapeDtypeStruct((B, S, D), BF16),
        scratch_shapes=[pltpu.VMEM((n_heads, HEAD, HEAD), F32)],
        compiler_params=_cparams(("parallel", "arbitrary")),
        name="rwkv_scan",
    )(r, k, v, lw, kn, a_sig, tri)


def _rwkv_out_kernel(x_ref, y_ref, r_ref, k_ref, v_ref, g_ref, lnw_ref, lnb_ref, rk_ref, bd_ref, wo_ref, o_ref):
    y = y_ref[0].astype(F32)
    inv_n = 1.0 / HEAD
    mean = _headsum(y, bd_ref) * inv_n
    d = y - mean
    var = _headsum(d * d, bd_ref) * inv_n
    yn = d * lax.rsqrt(var + RWKV_LN_EPS) * lnw_ref[...] + lnb_ref[...]
    rk = r_ref[0].astype(F32) * k_ref[0].astype(F32) * rk_ref[...]
    bonus = _headsum(rk, bd_ref) * v_ref[0].astype(F32)
    z = ((yn + bonus) * g_ref[0].astype(F32)).astype(BF16)
    o_ref[0] = x_ref[0] + jnp.dot(z, wo_ref[...], preferred_element_type=F32)


def _rwkv_out(x, y, r, k, v, g, ln_w, ln_b, r_k, w_o, *, tm=512):
    B, S, D = x.shape
    tile = pl.BlockSpec((1, tm, D), lambda b, i: (b, i, 0))
    vec = pl.BlockSpec((1, D), lambda b, i: (0, 0))
    return pl.pallas_call(
        _rwkv_out_kernel,
        grid=(B, S // tm),
        in_specs=[tile] * 6 + [vec, vec, vec, _vmem_spec(), _vmem_spec()],
        out_specs=tile,
        out_shape=jax.ShapeDtypeStruct((B, S, D), F32),
        compiler_params=_cparams(("parallel", "parallel")),
        name="rwkv_out",
    )(x, y, r, k, v, g, ln_w.reshape(1, D), ln_b.reshape(1, D), r_k.reshape(1, D),
      _block_diag_ones(), w_o.astype(BF16))


def _attn_qkv_kernel(x_ref, gain_ref, w_ref, qg_ref, kg_ref, bd_ref, q_o, k_o, v_o, *, dm):
    h = _rms(x_ref[0], gain_ref[...]).astype(BF16)
    qkv = jnp.dot(h, w_ref[...], preferred_element_type=F32)
    q, k, v = qkv[:, :dm], qkv[:, dm:2 * dm], qkv[:, 2 * dm:]
    inv_n = 1.0 / HEAD
    q = q * lax.rsqrt(_headsum(q * q, bd_ref) * inv_n + RMS_EPS) * qg_ref[...]
    k = k * lax.rsqrt(_headsum(k * k, bd_ref) * inv_n + RMS_EPS) * kg_ref[...]
    q_o[0, 0] = q.astype(BF16)
    k_o[0, 0] = k.astype(BF16)
    v_o[0, 0] = v.astype(BF16)


def _attn_qkv(x, gain, w, q_gain, k_gain, dilation):
    B, S, D = x.shape
    dm = w.shape[1] // 3
    sd = S // dilation
    tm = min(512, sd)
    n_heads = dm // HEAD
    xv = x.reshape(B, sd, dilation * D)
    scale = HEAD ** -0.5
    out = jax.ShapeDtypeStruct((B, dilation, sd, dm), BF16)
    ospec = pl.BlockSpec((1, 1, tm, dm), lambda b, r, j: (b, r, j, 0))
    vec = lambda n: pl.BlockSpec((1, n), lambda b, r, j: (0, 0))
    return pl.pallas_call(
        functools.partial(_attn_qkv_kernel, dm=dm),
        grid=(B, dilation, sd // tm),
        in_specs=[pl.BlockSpec((1, tm, D), lambda b, r, j: (b, j, r)), vec(D), _vmem_spec(),
                  vec(dm), vec(dm), _vmem_spec()],
        out_specs=[ospec] * 3,
        out_shape=[out] * 3,
        compiler_params=_cparams(("parallel", "parallel", "parallel")),
        name=f"attn_qkv_d{dilation}",
    )(xv, gain.reshape(1, D), w.astype(BF16), (jnp.tile(q_gain, n_heads) * scale).reshape(1, dm),
      jnp.tile(k_gain, n_heads).reshape(1, dm), _block_diag_ones())


def _attn_core_kernel(q_ref, kp_ref, kc_ref, vp_ref, vc_ref, bias_ref, o_ref, lse_ref, *, n_heads):
    q = q_ref[0, 0]
    kp, kc, vp, vc = kp_ref[0, 0], kc_ref[0, 0], vp_ref[0, 0], vc_ref[0, 0]
    col = lax.broadcasted_iota(jnp.int32, (BAND, 2 * BAND), 1)
    key_ok = (col >= BAND) | (pl.program_id(2) > 0)
    lane = lax.broadcasted_iota(jnp.int32, (BAND, LSE_W), 1)
    lse_pad = jnp.zeros((BAND, LSE_W), F32)
    for hd in range(n_heads):
        sl = slice(hd * HEAD, (hd + 1) * HEAD)
        kh = jnp.concatenate([kp[:, sl], kc[:, sl]], axis=0)
        vh = jnp.concatenate([vp[:, sl], vc[:, sl]], axis=0)
        logits = lax.dot_general(q[:, sl], kh, (((1,), (1,)), ((), ())), preferred_element_type=F32)
        logits = jnp.where(key_ok, logits + bias_ref[hd], NEG_INF)
        m = jnp.max(logits, axis=-1, keepdims=True)
        p = jnp.exp(logits - m)
        l = jnp.sum(p, axis=-1, keepdims=True)
        o = jnp.dot(p.astype(BF16), vh, preferred_element_type=F32) / l
        o_ref[0, :, sl] = o.astype(o_ref.dtype)
        lse_pad = jnp.where(lane == hd, m + jnp.log(l), lse_pad)
    lse_ref[0] = lse_pad


def _attn_core(q, k, v, bias, dilation):
    B, d, sd, dm = q.shape
    n_heads = dm // HEAD
    nb = sd // BAND
    cur = pl.BlockSpec((1, 1, BAND, dm), lambda b, r, n: (b, r, n, 0))
    prv = pl.BlockSpec((1, 1, BAND, dm), lambda b, r, n: (b, r, jnp.maximum(n - 1, 0), 0))
    o, lse = pl.pallas_call(
        functools.partial(_attn_core_kernel, n_heads=n_heads),
        grid=(B, d, nb),
        in_specs=[cur, prv, cur, prv, cur, _vmem_spec()],
        out_specs=[pl.BlockSpec((1, BAND, dm), lambda b, r, n: (b, n, r)),
                   pl.BlockSpec((1, BAND, LSE_W), lambda b, r, n: (b, n, r))],
        out_shape=[jax.ShapeDtypeStruct((B, sd, d * dm), BF16),
                   jax.ShapeDtypeStruct((B, sd, d * LSE_W), F32)],
        compiler_params=_cparams(("parallel", "parallel", "parallel")),
        name=f"attn_core_d{dilation}",
    )(q, k, k, v, v, bias)
    return o.reshape(B, sd * d, dm), lse.reshape(B, sd * d, LSE_W)


def _attn_out_kernel(x_ref, o1_ref, o2_ref, o3_ref, l1_ref, l2_ref, l3_ref, ex_ref, wo_ref, out_ref):
    ls = [l1_ref[0], l2_ref[0], l3_ref[0]]
    os_ = [o1_ref[0], o2_ref[0], o3_ref[0]]
    m = jnp.maximum(jnp.maximum(ls[0], ls[1]), ls[2])
    es = [jnp.exp(l - m) for l in ls]
    inv = 1.0 / (es[0] + es[1] + es[2])
    acc = None
    for e, o in zip(es, os_):
        wgt = e * inv
        hi = wgt.astype(BF16)
        lo = (wgt - hi.astype(F32)).astype(BF16)
        wfull = (jnp.dot(hi, ex_ref[...], preferred_element_type=F32)
                 + jnp.dot(lo, ex_ref[...], preferred_element_type=F32))
        term = wfull * o.astype(F32)
        acc = term if acc is None else acc + term
    out_ref[0] = x_ref[0] + jnp.dot(acc.astype(BF16), wo_ref[...], preferred_element_type=F32)


def _attn_out(x, outs, lses, w_o, *, tm=512):
    B, S, D = x.shape
    dm = w_o.shape[0]
    n_heads = dm // HEAD
    expand = np.zeros((LSE_W, dm), np.float32)
    for hd in range(n_heads):
        expand[hd, hd * HEAD:(hd + 1) * HEAD] = 1.0
    tile = lambda w: pl.BlockSpec((1, tm, w), lambda b, i: (b, i, 0))
    return pl.pallas_call(
        _attn_out_kernel,
        grid=(B, S // tm),
        in_specs=[tile(D)] + [tile(dm)] * 3 + [tile(LSE_W)] * 3 + [_vmem_spec(), _vmem_spec()],
        out_specs=tile(D),
        out_shape=jax.ShapeDtypeStruct((B, S, D), F32),
        compiler_params=_cparams(("parallel", "parallel")),
        name="attn_out",
    )(x, *outs, *lses, jnp.asarray(expand, dtype=BF16), w_o.astype(BF16))


def _t5_bucket(dist):
    max_exact = NUM_BUCKETS // 2
    large = max_exact + (jnp.log(jnp.maximum(dist, 1).astype(F32) / max_exact)
                         / math.log(MAX_DISTANCE / max_exact) * (NUM_BUCKETS - max_exact)).astype(jnp.int32)
    large = jnp.minimum(large, NUM_BUCKETS - 1)
    return jnp.where(dist < max_exact, dist, large)


def _band_bias(table_g, window, dilation):
    i = jnp.arange(BAND)[:, None]
    c = jnp.arange(2 * BAND)[None, :]
    steps = i + BAND - c
    allowed = (steps >= 0) & (steps <= window // dilation)
    bucket = _t5_bucket(jnp.maximum(steps, 0) * dilation)
    bias = jnp.take(table_g, bucket, axis=0).astype(F32)
    bias = jnp.where(allowed[..., None], bias, NEG_INF)
    return jnp.transpose(bias, (2, 0, 1))


def _attn_mixer(x, gain, w_qkv, q_gain, k_gain, w_o, rel_bias):
    dm = w_o.shape[0]
    n_heads = dm // HEAD
    outs, lses = [], []
    for g, (window, dilation) in enumerate(DILATION_PAIRS):
        wg = w_qkv[:, g * 3 * dm:(g + 1) * 3 * dm]
        q, k, v = _attn_qkv(x, gain, wg, q_gain[g], k_gain[g], dilation)
        bias = _band_bias(rel_bias[:, g * n_heads:(g + 1) * n_heads], window, dilation)
        o, lse = _attn_core(q, k, v, bias, dilation)
        outs.append(o)
        lses.append(lse)
    return _attn_out(x, outs, lses, w_o)


def kernel(x, mix_norm, ffn_norm, ffn_w_in, ffn_w_out, rwkv_mu, rwkv_w_rkv, rwkv_w0, rwkv_w1, rwkv_w2, rwkv_a0, rwkv_a1, rwkv_a2, rwkv_v0, rwkv_v1, rwkv_v2, rwkv_g1, rwkv_g2, rwkv_k_k, rwkv_k_a, rwkv_r_k, rwkv_ln_w, rwkv_ln_b, rwkv_w_o, attn_w_qkv, attn_q_gain, attn_k_gain, attn_w_o, rel_bias):
    depth = mix_norm.shape[0]
    v_first = None
    for layer in range(depth):
        i = layer // 2
        if layer % 2 == 0:
            v_gate = None if i == 0 else (rwkv_v0[i - 1], rwkv_v1[i - 1], rwkv_v2[i - 1])
            r, k, v, lw, kn, a_sig, g = _rwkv_proj(
                x, mix_norm[layer], rwkv_mu[i], rwkv_w_rkv[i], rwkv_w0[i], rwkv_w1[i], rwkv_w2[i],
                rwkv_a0[i], rwkv_a1[i], rwkv_a2[i], rwkv_g1[i], rwkv_g2[i], rwkv_k_k[i], rwkv_k_a[i],
                v_gate, v_first)
            if v_first is None:
                v_first = v
            y = _rwkv_scan(r, k, v, lw, kn, a_sig)
            x = _rwkv_out(x, y, r, k, v, g, rwkv_ln_w[i], rwkv_ln_b[i], rwkv_r_k[i], rwkv_w_o[i])
        else:
            x = _attn_mixer(x, mix_norm[layer], attn_w_qkv[i], attn_q_gain[i], attn_k_gain[i],
                            attn_w_o[i], rel_bias)
        x = _ffn(x, ffn_norm[layer], ffn_w_in[layer], ffn_w_out[layer])
    return x
```

```python
import functools
import math

import numpy as np
import jax
import jax.numpy as jnp
from jax import lax
from jax.experimental import pallas as pl
from jax.experimental.pallas import tpu as pltpu

F32 = jnp.float32
BF16 = jnp.bfloat16

HEAD = 64
LANES = 128
PAIR = 2 * HEAD
RMS_EPS = 1e-6
RWKV_LN_EPS = 64e-5
DILATION_PAIRS = ((128, 1), (512, 4), (2048, 16))
BAND = 128
NUM_BUCKETS = 32
MAX_DISTANCE = 2048
NEG_INF = -1e30
CHUNK = 64
SCAN_ROWS = 2
HEADSUM_W = 256
LSE_W = 128
TOKEN_TILE = 512
FFN_TILE = 256
VMEM_LIMIT = 56 * 1024 * 1024


def _cparams(sem):
    return pltpu.CompilerParams(dimension_semantics=sem, vmem_limit_bytes=VMEM_LIMIT)


def _vmem_spec():
    return pl.BlockSpec(memory_space=pltpu.VMEM)


def _rms(x, gain):
    return x * lax.rsqrt(jnp.mean(x * x, axis=-1, keepdims=True) + RMS_EPS) * gain


def _sigmoid(x):
    return 1.0 / (1.0 + jnp.exp(-x))


def _dot(a, b):
    return jnp.dot(a.astype(BF16), b.astype(BF16), preferred_element_type=F32)


def _dot_nt(a, b):
    return lax.dot_general(a.astype(BF16), b.astype(BF16), (((1,), (1,)), ((), ())),
                           preferred_element_type=F32)


def _dot_tn(a, b):
    return lax.dot_general(a.astype(BF16), b.astype(BF16), (((0,), (0,)), ((), ())),
                           preferred_element_type=F32)


def _headsum(x, bd_ref):
    d = x.shape[-1]
    bd = bd_ref[...]
    parts = [_dot(x[:, c:c + HEADSUM_W], bd) for c in range(0, d, HEADSUM_W)]
    return jnp.concatenate(parts, axis=-1)


def _block_diag_ones():
    i = np.arange(HEADSUM_W) // HEAD
    return jnp.asarray((i[:, None] == i[None, :]).astype(np.float32), dtype=BF16)


def _ffn_tile(x, g_ref, win_ref, wout_ref):
    d_ff = wout_ref.shape[0]
    h = _rms(x, g_ref[...]).astype(BF16)
    acc = x
    for c in range(0, d_ff, FFN_TILE):
        gate = jnp.dot(h, win_ref[:, c:c + FFN_TILE], preferred_element_type=F32)
        up = jnp.dot(h, win_ref[:, d_ff + c:d_ff + c + FFN_TILE], preferred_element_type=F32)
        act = (gate * _sigmoid(gate) * up).astype(BF16)
        acc = acc + jnp.dot(act, wout_ref[c:c + FFN_TILE, :], preferred_element_type=F32)
    return acc


def _rwkv_proj_kernel(*refs, has_vgate):
    if has_vgate:
        (x_ref, xp_ref, gain_ref, mu_ref, wrkv_ref, w0_ref, w1_ref, w2_ref, a0_ref, a1_ref, a2_ref,
         g1_ref, g2_ref, kk_ref, ka_ref, bd_ref, v0_ref, v1_ref, v2_ref, vf_ref,
         r_o, k_o, v_o, lw_o, kn_o, as_o, g_o) = refs
    else:
        (x_ref, xp_ref, gain_ref, mu_ref, wrkv_ref, w0_ref, w1_ref, w2_ref, a0_ref, a1_ref, a2_ref,
         g1_ref, g2_ref, kk_ref, ka_ref, bd_ref,
         r_o, k_o, v_o, lw_o, kn_o, as_o, g_o) = refs
    gain = gain_ref[...]
    h = _rms(x_ref[0], gain)
    hp_row = _rms(xp_ref[0], gain)[7:8, :]
    hp_row = jnp.where(pl.program_id(1) == 0, 0.0, hp_row)
    row = lax.broadcasted_iota(jnp.int32, h.shape, 0)
    h_prev = jnp.where(row == 0, hp_row, pltpu.roll(h, 1, axis=0))
    xx = h_prev - h

    def mix(c):
        return (h + xx * mu_ref[c:c + 1, :]).astype(BF16)

    xr, xk, xv, xw, xa, xg = (mix(c) for c in range(6))
    r = jnp.dot(xr, wrkv_ref[0], preferred_element_type=F32)
    k = jnp.dot(xk, wrkv_ref[1], preferred_element_type=F32)
    v = jnp.dot(xv, wrkv_ref[2], preferred_element_type=F32)

    wl = w0_ref[...] + _dot(jnp.tanh(jnp.dot(xw, w1_ref[...], preferred_element_type=F32)), w2_ref[...])
    w = -(jnp.maximum(-wl, 0.0) + jnp.log(1.0 + jnp.exp(-jnp.abs(wl)))) - 0.5
    lw_o[0] = -jnp.exp(w)
    a = _sigmoid(a0_ref[...] + _dot(jnp.dot(xa, a1_ref[...], preferred_element_type=F32), a2_ref[...]))
    g = _dot(_sigmoid(jnp.dot(xg, g1_ref[...], preferred_element_type=F32)), g2_ref[...])
    if has_vgate:
        gate = _sigmoid(v0_ref[...] + _dot(jnp.dot(xv, v1_ref[...], preferred_element_type=F32), v2_ref[...]))
        v = v + (vf_ref[0].astype(F32) - v) * gate

    kk = k * kk_ref[...]
    kn = kk * lax.rsqrt(jnp.maximum(_headsum(kk * kk, bd_ref), 1e-24))
    k = k * (1.0 + (a - 1.0) * ka_ref[...])
    r_o[0] = r.astype(BF16)
    k_o[0] = k.astype(BF16)
    v_o[0] = v.astype(BF16)
    kn_o[0] = kn.astype(BF16)
    as_o[0] = a.astype(BF16)
    g_o[0] = g.astype(BF16)


def _rwkv_proj(x, gain, mu, w_rkv, w0, w1, w2, a0, a1, a2, g1, g2, k_k, k_a, v_gate, v_first):
    B, S, D = x.shape
    tm = TOKEN_TILE
    has_vgate = v_gate is not None
    row = lambda t: t.reshape(1, D)
    tile = pl.BlockSpec((1, tm, D), lambda b, i: (b, i, 0))
    prev = pl.BlockSpec((1, 8, D), lambda b, i: (b, jnp.maximum(i * (tm // 8) - 1, 0), 0))
    vec = pl.BlockSpec((1, D), lambda b, i: (0, 0))
    args = [x, x, row(gain), mu, w_rkv.astype(BF16), row(w0), w1.astype(BF16), w2.astype(BF16),
            row(a0), a1.astype(BF16), a2.astype(BF16), g1.astype(BF16), g2.astype(BF16),
            row(k_k), row(k_a), _block_diag_ones()]
    specs = [tile, prev, vec, _vmem_spec(), _vmem_spec(), vec, _vmem_spec(), _vmem_spec(),
             vec, _vmem_spec(), _vmem_spec(), _vmem_spec(), _vmem_spec(), vec, vec, _vmem_spec()]
    if has_vgate:
        v0, v1, v2 = v_gate
        args += [row(v0), v1.astype(BF16), v2.astype(BF16), v_first]
        specs += [vec, _vmem_spec(), _vmem_spec(), tile]
    bshape = jax.ShapeDtypeStruct((B, S, D), BF16)
    out_shape = [bshape, bshape, bshape, jax.ShapeDtypeStruct((B, S, D), F32), bshape, bshape, bshape]
    return pl.pallas_call(
        functools.partial(_rwkv_proj_kernel, has_vgate=has_vgate),
        grid=(B, S // tm),
        in_specs=specs,
        out_specs=[tile] * 7,
        out_shape=out_shape,
        compiler_params=_cparams(("parallel", "parallel")),
        name="rwkv_proj",
    )(*args)


def _rwkv_scan_kernel(r_ref, k_ref, v_ref, lw_ref, kn_ref, as_ref, tri_ref, y_ref, state_ref, *,
                      n_pairs, n_rows):
    C = CHUNK

    @pl.when(pl.program_id(1) == 0)
    def _():
        state_ref[...] = jnp.zeros_like(state_ref)

    tri = tri_ref[...]
    r_d, a_d, k_d, b_d, k_e, b_e, w_tot, v = ([] for _ in range(8))
    for bi in range(n_rows):
        lw = lw_ref[bi]
        lw_hi = lw.astype(BF16)
        rem = lw - lw_hi.astype(F32)
        lw_mid = rem.astype(BF16)
        lw_lo = (rem - lw_mid.astype(F32)).astype(BF16)
        cum = (jnp.dot(tri, lw_hi, preferred_element_type=F32) + jnp.dot(tri, lw_mid, preferred_element_type=F32)
               + jnp.dot(tri, lw_lo, preferred_element_type=F32))
        tot = cum[C - 1:C, :]
        r = r_ref[bi].astype(F32)
        k = k_ref[bi].astype(F32)
        kn = kn_ref[bi].astype(F32)
        b = kn * as_ref[bi].astype(F32)
        w_inv = jnp.exp(-cum)
        w_end = jnp.exp(tot - cum)
        r_d.append((r * jnp.exp(cum)).astype(BF16))
        a_d.append((-kn * jnp.exp(cum - lw)).astype(BF16))
        k_d.append((k * w_inv).astype(BF16))
        b_d.append((b * w_inv).astype(BF16))
        k_e.append((k * w_end).astype(BF16))
        b_e.append((b * w_end).astype(BF16))
        w_tot.append(jnp.exp(tot))
        v.append(v_ref[bi])

    ri = lax.broadcasted_iota(jnp.int32, (C, PAIR), 0)
    lane = lax.broadcasted_iota(jnp.int32, (C, PAIR), 1)
    even = lane < HEAD
    ci = jnp.where(even, lane, lane - HEAD)
    strict = ri > ci
    incl = ri >= ci
    eye = jnp.where(ri == ci, 1.0, 0.0).astype(F32)

    def bdiag(yp):
        yp = yp.astype(BF16)
        zero = jnp.zeros_like(yp)
        return jnp.concatenate([jnp.where(even, yp, zero), jnp.where(even, zero, yp)], axis=0)

    ch = [(bi, j) for bi in range(n_rows) for j in range(n_pairs)]
    n = range(len(ch))
    sls = [slice(j * PAIR, (j + 1) * PAIR) for _, j in ch]
    pick = lambda arrs, c: arrs[ch[c][0]][:, sls[c]]
    xa_r = [jnp.concatenate([pick(a_d, c), pick(r_d, c)], axis=0) for c in n]
    g_k = [_dot_nt(xa_r[c], bdiag(pick(k_d, c))) for c in n]
    g_b = [_dot_nt(xa_r[c], bdiag(pick(b_d, c))) for c in n]
    a_ab = [jnp.where(strict, g_b[c][:C], 0.0) for c in n]
    a_kk = [jnp.concatenate([jnp.where(strict, g_k[c][:C], 0.0), jnp.where(incl, g_k[c][C:], 0.0)],
                            axis=0).astype(BF16) for c in n]
    a_rb = [jnp.where(incl, g_b[c][C:], 0.0).astype(BF16) for c in n]
    p = [eye + a_ab[c] for c in n]
    q = [_dot(a_ab[c], bdiag(a_ab[c])) for c in n]
    for _ in range(int(math.log2(C)) - 2):
        qp = [_dot(jnp.concatenate([q[c], p[c]], axis=0), bdiag(q[c])) for c in n]
        q = [qp[c][:C] for c in n]
        p = [p[c] + qp[c][C:] for c in n]
    p = [p[c] + _dot(p[c], bdiag(q[c])) for c in n]
    s0 = [state_ref[bi, j] for bi, j in ch]
    xs = [_dot_nt(xa_r[c], bdiag(s0[c])) for c in n]
    av = [_dot(a_kk[c], bdiag(pick(v, c))) for c in n]
    u = [_dot(p[c], bdiag(xs[c][:C] + av[c][:C])) for c in n]
    y = [xs[c][C:] + av[c][C:] + _dot(a_rb[c], bdiag(u[c])) for c in n]
    for c in n:
        bi, j = ch[c]
        y_ref[bi, :, sls[c]] = y[c].astype(y_ref.dtype)
        vu = jnp.concatenate([pick(v, c), u[c].astype(BF16)], axis=0)
        kb = jnp.concatenate([pick(k_e, c), pick(b_e, c)], axis=0)
        upd = _dot_tn(vu, kb)
        state_ref[bi, j] = s0[c] * pick(w_tot, c) + jnp.where(even, upd[:HEAD], upd[HEAD:])


def _rwkv_scan(r, k, v, lw, kn, a_sig):
    B, S, D = r.shape
    n_pairs = D // PAIR
    n_rows = math.gcd(B, SCAN_ROWS)
    tile = pl.BlockSpec((n_rows, CHUNK, D), lambda b, c: (b, c, 0))
    tri = jnp.asarray(np.tril(np.ones((CHUNK, CHUNK), np.float32)), dtype=BF16)
    return pl.pallas_call(
        functools.partial(_rwkv_scan_kernel, n_pairs=n_pairs, n_rows=n_rows),
        grid=(B // n_rows, S // CHUNK),
        in_specs=[tile] * 6 + [pl.BlockSpec((CHUNK, CHUNK), lambda b, c: (0, 0))],
        out_specs=tile,
        out_shape=jax.ShapeDtypeStruct((B, S, D), BF16),
        scratch_shapes=[pltpu.VMEM((n_rows, n_pairs, HEAD, PAIR), F32)],
        compiler_params=_cparams(("parallel", "arbitrary")),
        name="rwkv_scan",
    )(r, k, v, lw, kn, a_sig, tri)


def _rwkv_out_kernel(x_ref, y_ref, r_ref, k_ref, v_ref, g_ref, lnw_ref, lnb_ref, rk_ref, bd_ref, wo_ref,
                     fg_ref, win_ref, wout_ref, o_ref):
    y = y_ref[0].astype(F32)
    inv_n = 1.0 / HEAD
    mean = _headsum(y, bd_ref) * inv_n
    d = y - mean
    var = _headsum(d * d, bd_ref) * inv_n
    yn = d * lax.rsqrt(var + RWKV_LN_EPS) * lnw_ref[...] + lnb_ref[...]
    rk = r_ref[0].astype(F32) * k_ref[0].astype(F32) * rk_ref[...]
    bonus = _headsum(rk, bd_ref) * v_ref[0].astype(F32)
    z = ((yn + bonus) * g_ref[0].astype(F32)).astype(BF16)
    x2 = x_ref[0] + jnp.dot(z, wo_ref[...], preferred_element_type=F32)
    o_ref[0] = _ffn_tile(x2, fg_ref, win_ref, wout_ref)


def _rwkv_out(x, y, r, k, v, g, ln_w, ln_b, r_k, w_o, ffn_gain, w_in, w_out):
    B, S, D = x.shape
    tm = TOKEN_TILE
    tile = pl.BlockSpec((1, tm, D), lambda b, i: (b, i, 0))
    vec = pl.BlockSpec((1, D), lambda b, i: (0, 0))
    return pl.pallas_call(
        _rwkv_out_kernel,
        grid=(B, S // tm),
        in_specs=[tile] * 6 + [vec, vec, vec, _vmem_spec(), _vmem_spec(), vec, _vmem_spec(), _vmem_spec()],
        out_specs=tile,
        out_shape=jax.ShapeDtypeStruct((B, S, D), F32),
        compiler_params=_cparams(("parallel", "parallel")),
        name="rwkv_out_ffn",
    )(x, y, r, k, v, g, ln_w.reshape(1, D), ln_b.reshape(1, D), r_k.reshape(1, D),
      _block_diag_ones(), w_o.astype(BF16), ffn_gain.reshape(1, D), w_in.astype(BF16), w_out.astype(BF16))


def _attn_qkv_kernel(x_ref, gain_ref, w_ref, qg_ref, kg_ref, bd_ref, *refs, dm, dils):
    outs, h_scr = refs[:-1], refs[-1]
    tm = x_ref.shape[1]
    h = _rms(x_ref[0], gain_ref[...])
    n_col = h.shape[1] // LANES
    for c in range(n_col):
        h_scr[c] = h[:, c * LANES:(c + 1) * LANES]
    inv_n = 1.0 / HEAD
    for g, d in enumerate(dils):
        n = tm // d
        if d == 1:
            hp = h
        else:
            hp = jnp.concatenate(
                [jnp.concatenate([h_scr[c, pl.ds(r, n, stride=d), :] for c in range(n_col)], axis=1)
                 for r in range(d)], axis=0)
        qkv = jnp.dot(hp.astype(BF16), w_ref[:, g * 3 * dm:(g + 1) * 3 * dm], preferred_element_type=F32)
        q, k, v = qkv[:, :dm], qkv[:, dm:2 * dm], qkv[:, 2 * dm:]
        q = q * lax.rsqrt(_headsum(q * q, bd_ref) * inv_n + RMS_EPS) * qg_ref[g:g + 1, :]
        k = k * lax.rsqrt(_headsum(k * k, bd_ref) * inv_n + RMS_EPS) * kg_ref[g:g + 1, :]
        for t, o_ref in zip((q, k, v), outs[3 * g:3 * g + 3]):
            t = t.astype(BF16)
            for r in range(d):
                o_ref[0, r] = t[r * n:(r + 1) * n]


def _attn_qkv(x, gain, w, q_gain, k_gain):
    B, S, D = x.shape
    dm = w.shape[1] // (3 * len(DILATION_PAIRS))
    n_heads = dm // HEAD
    tm = TOKEN_TILE
    dils = tuple(d for _, d in DILATION_PAIRS)
    scale = HEAD ** -0.5
    out_shape, out_specs = [], []
    for d in dils:
        for _ in range(3):
            out_shape.append(jax.ShapeDtypeStruct((B, d, S // d, dm), BF16))
            out_specs.append(pl.BlockSpec((1, d, tm // d, dm), lambda b, i: (b, 0, i, 0)))
    vec = lambda rows, n: pl.BlockSpec((rows, n), lambda b, i: (0, 0))
    return pl.pallas_call(
        functools.partial(_attn_qkv_kernel, dm=dm, dils=dils),
        grid=(B, S // tm),
        in_specs=[pl.BlockSpec((1, tm, D), lambda b, i: (b, i, 0)), vec(1, D), _vmem_spec(),
                  vec(len(dils), dm), vec(len(dils), dm), _vmem_spec()],
        out_specs=out_specs,
        out_shape=out_shape,
        scratch_shapes=[pltpu.VMEM((D // LANES, tm, LANES), F32)],
        compiler_params=_cparams(("parallel", "parallel")),
        name="attn_qkv",
    )(x, gain.reshape(1, D), w.astype(BF16), jnp.tile(q_gain, (1, n_heads)) * scale,
      jnp.tile(k_gain, (1, n_heads)), _block_diag_ones())


def _attn_core_kernel(q_ref, kp_ref, kc_ref, vp_ref, vc_ref, bias_ref, o_ref, lse_ref, *, n_heads):
    q = q_ref[0, 0]
    kp, kc, vp, vc = kp_ref[0, 0], kc_ref[0, 0], vp_ref[0, 0], vc_ref[0, 0]
    tbl = jnp.minimum(pl.program_id(2), 1)
    lane_q = lax.broadcasted_iota(jnp.int32, (BAND, PAIR), 1)
    even = lane_q < HEAD
    lane = lax.broadcasted_iota(jnp.int32, (BAND, LSE_W), 1)
    lse_pad = jnp.zeros((BAND, LSE_W), F32)
    zero = jnp.zeros((BAND, PAIR), BF16)
    group = 2
    for j0 in range(0, n_heads // 2, group):
        pairs = range(j0, j0 + group)
        logits = {}
        for j in pairs:
            sl = slice(j * PAIR, (j + 1) * PAIR)
            keys = jnp.concatenate([kp[:, sl], kc[:, sl]], axis=0)
            qp = q[:, sl]
            for hd, qm in ((2 * j, jnp.where(even, qp, zero)), (2 * j + 1, jnp.where(even, zero, qp))):
                s = lax.dot_general(qm, keys, (((1,), (1,)), ((), ())), preferred_element_type=F32)
                logits[hd] = s + bias_ref[tbl, hd]
        m, l, p = {}, {}, {}
        for hd in logits:
            m[hd] = jnp.max(logits[hd], axis=-1, keepdims=True)
            e = jnp.exp(logits[hd] - m[hd])
            l[hd] = jnp.sum(e, axis=-1, keepdims=True)
            p[hd] = e.astype(BF16)
        for j in pairs:
            sl = slice(j * PAIR, (j + 1) * PAIR)
            vals = jnp.concatenate([vp[:, sl], vc[:, sl]], axis=0)
            o_e = jnp.dot(p[2 * j], vals, preferred_element_type=F32) / l[2 * j]
            o_o = jnp.dot(p[2 * j + 1], vals, preferred_element_type=F32) / l[2 * j + 1]
            o_ref[0, 0, :, sl] = jnp.where(even, o_e, o_o).astype(o_ref.dtype)
            for hd in (2 * j, 2 * j + 1):
                lse_pad = jnp.where(lane == hd, m[hd] + jnp.log(l[hd]), lse_pad)
    lse_ref[0, 0] = lse_pad


def _attn_core(q, k, v, bias, dilation):
    B, d, sd, dm = q.shape
    n_heads = dm // HEAD
    nb = sd // BAND
    cur = pl.BlockSpec((1, 1, BAND, dm), lambda b, r, n: (b, r, n, 0))
    prv = pl.BlockSpec((1, 1, BAND, dm), lambda b, r, n: (b, r, jnp.maximum(n - 1, 0), 0))
    return pl.pallas_call(
        functools.partial(_attn_core_kernel, n_heads=n_heads),
        grid=(B, d, nb),
        in_specs=[cur, prv, cur, prv, cur, _vmem_spec()],
        out_specs=[cur, pl.BlockSpec((1, 1, BAND, LSE_W), lambda b, r, n: (b, r, n, 0))],
        out_shape=[jax.ShapeDtypeStruct((B, d, sd, dm), BF16),
                   jax.ShapeDtypeStruct((B, d, sd, LSE_W), F32)],
        compiler_params=_cparams(("parallel", "parallel", "parallel")),
        name=f"attn_core_d{dilation}",
    )(q, k, k, v, v, bias)


def _attn_out_kernel(x_ref, o1_ref, o2_ref, o3_ref, l1_ref, l2_ref, l3_ref, ex_ref, wo_ref,
                     fg_ref, win_ref, wout_ref, out_ref, o_scr, l_scr):
    tm = x_ref.shape[1]

    def token_order(ref, scr):
        d = ref.shape[1]
        if d == 1:
            return ref[0, 0].astype(F32)
        n_col = ref.shape[3] // LANES
        for r in range(d):
            blk = ref[0, r].astype(F32)
            for c in range(n_col):
                scr[c, pl.ds(r, tm // d, stride=d), :] = blk[:, c * LANES:(c + 1) * LANES]
        return jnp.concatenate([scr[c] for c in range(n_col)], axis=1)

    ls = [token_order(l_ref, l_scr.at[pl.ds(i, 1)]) for i, l_ref in enumerate((l1_ref, l2_ref, l3_ref))]
    m = jnp.maximum(jnp.maximum(ls[0], ls[1]), ls[2])
    es = [jnp.exp(l - m) for l in ls]
    inv = 1.0 / (es[0] + es[1] + es[2])
    acc = None
    for e, o_ref in zip(es, (o1_ref, o2_ref, o3_ref)):
        wgt = e * inv
        hi = wgt.astype(BF16)
        lo = (wgt - hi.astype(F32)).astype(BF16)
        wfull = (jnp.dot(hi, ex_ref[...], preferred_element_type=F32)
                 + jnp.dot(lo, ex_ref[...], preferred_element_type=F32))
        term = wfull * token_order(o_ref, o_scr)
        acc = term if acc is None else acc + term
    x2 = x_ref[0] + jnp.dot(acc.astype(BF16), wo_ref[...], preferred_element_type=F32)
    out_ref[0] = _ffn_tile(x2, fg_ref, win_ref, wout_ref)


def _attn_out(x, outs, lses, w_o, ffn_gain, w_in, w_out):
    B, S, D = x.shape
    dm = w_o.shape[0]
    n_heads = dm // HEAD
    tm = TOKEN_TILE
    expand = np.zeros((LSE_W, dm), np.float32)
    for hd in range(n_heads):
        expand[hd, hd * HEAD:(hd + 1) * HEAD] = 1.0
    tile = pl.BlockSpec((1, tm, D), lambda b, i: (b, i, 0))
    vec = pl.BlockSpec((1, D), lambda b, i: (0, 0))
    grouped = lambda t: pl.BlockSpec((1, t.shape[1], tm // t.shape[1], t.shape[3]), lambda b, i: (b, 0, i, 0))
    return pl.pallas_call(
        _attn_out_kernel,
        grid=(B, S // tm),
        in_specs=[tile] + [grouped(t) for t in outs] + [grouped(t) for t in lses]
                 + [_vmem_spec(), _vmem_spec(), vec, _vmem_spec(), _vmem_spec()],
        out_specs=tile,
        out_shape=jax.ShapeDtypeStruct((B, S, D), F32),
        scratch_shapes=[pltpu.VMEM((dm // LANES, tm, LANES), F32), pltpu.VMEM((len(lses), tm, LSE_W), F32)],
        compiler_params=_cparams(("parallel", "parallel")),
        name="attn_out_ffn",
    )(x, *outs, *lses, jnp.asarray(expand, dtype=BF16), w_o.astype(BF16),
      ffn_gain.reshape(1, D), w_in.astype(BF16), w_out.astype(BF16))


def _t5_bucket(dist):
    max_exact = NUM_BUCKETS // 2
    large = max_exact + (jnp.log(jnp.maximum(dist, 1).astype(F32) / max_exact)
                         / math.log(MAX_DISTANCE / max_exact) * (NUM_BUCKETS - max_exact)).astype(jnp.int32)
    large = jnp.minimum(large, NUM_BUCKETS - 1)
    return jnp.where(dist < max_exact, dist, large)


def _band_bias(table_g, window, dilation):
    i = jnp.arange(BAND)[:, None]
    c = jnp.arange(2 * BAND)[None, :]
    steps = i + BAND - c
    allowed = (steps >= 0) & (steps <= window // dilation)
    bucket = _t5_bucket(jnp.maximum(steps, 0) * dilation)
    bias = jnp.take(table_g, bucket, axis=0).astype(F32)
    bias = jnp.transpose(jnp.where(allowed[..., None], bias, NEG_INF), (2, 0, 1))
    first = jnp.where((c >= BAND)[None], bias, NEG_INF)
    return jnp.stack([first, bias])


def _attn_mixer(x, gain, w_qkv, q_gain, k_gain, w_o, rel_bias, ffn_gain, w_in, w_out):
    dm = w_o.shape[0]
    n_heads = dm // HEAD
    qkv = _attn_qkv(x, gain, w_qkv, q_gain, k_gain)
    outs, lses = [], []
    for g, (window, dilation) in enumerate(DILATION_PAIRS):
        q, k, v = qkv[3 * g:3 * g + 3]
        bias = _band_bias(rel_bias[:, g * n_heads:(g + 1) * n_heads], window, dilation)
        o, lse = _attn_core(q, k, v, bias, dilation)
        outs.append(o)
        lses.append(lse)
    return _attn_out(x, outs, lses, w_o, ffn_gain, w_in, w_out)


def kernel(x, mix_norm, ffn_norm, ffn_w_in, ffn_w_out, rwkv_mu, rwkv_w_rkv, rwkv_w0, rwkv_w1, rwkv_w2, rwkv_a0, rwkv_a1, rwkv_a2, rwkv_v0, rwkv_v1, rwkv_v2, rwkv_g1, rwkv_g2, rwkv_k_k, rwkv_k_a, rwkv_r_k, rwkv_ln_w, rwkv_ln_b, rwkv_w_o, attn_w_qkv, attn_q_gain, attn_k_gain, attn_w_o, rel_bias):
    depth = mix_norm.shape[0]
    v_first = None
    for layer in range(depth):
        i = layer // 2
        ffn = (ffn_norm[layer], ffn_w_in[layer], ffn_w_out[layer])
        if layer % 2 == 0:
            v_gate = None if i == 0 else (rwkv_v0[i - 1], rwkv_v1[i - 1], rwkv_v2[i - 1])
            r, k, v, lw, kn, a_sig, g = _rwkv_proj(
                x, mix_norm[layer], rwkv_mu[i], rwkv_w_rkv[i], rwkv_w0[i], rwkv_w1[i], rwkv_w2[i],
                rwkv_a0[i], rwkv_a1[i], rwkv_a2[i], rwkv_g1[i], rwkv_g2[i], rwkv_k_k[i], rwkv_k_a[i],
                v_gate, v_first)
            if v_first is None:
                v_first = v
            y = _rwkv_scan(r, k, v, lw, kn, a_sig)
            x = _rwkv_out(x, y, r, k, v, g, rwkv_ln_w[i], rwkv_ln_b[i], rwkv_r_k[i], rwkv_w_o[i], *ffn)
        else:
            x = _attn_mixer(x, mix_norm[layer], attn_w_qkv[i], attn_q_gain[i], attn_k_gain[i],
                            attn_w_o[i], rel_bias, *ffn)
    return x
```

```python
import functools
import math

import numpy as np
import jax
import jax.numpy as jnp
from jax import lax
from jax.experimental import pallas as pl
from jax.experimental.pallas import tpu as pltpu

F32 = jnp.float32
BF16 = jnp.bfloat16

HEAD = 64
LANES = 128
PAIR = 2 * HEAD
RMS_EPS = 1e-6
RWKV_LN_EPS = 64e-5
DILATION_PAIRS = ((128, 1), (512, 4), (2048, 16))
BAND = 128
NUM_BUCKETS = 32
MAX_DISTANCE = 2048
NEG_INF = -1e30
CHUNK = 64
SCAN_ROWS = 4
HEADSUM_W = 256
LSE_W = 128
CORE_GROUP = 2
LOG2E = math.log2(math.e)
LN2 = math.log(2.0)
TOKEN_TILE = 512
FFN_TILE = 256
VMEM_LIMIT = 56 * 1024 * 1024


def _cparams(sem):
    return pltpu.CompilerParams(dimension_semantics=sem, vmem_limit_bytes=VMEM_LIMIT)


def _vmem_spec():
    return pl.BlockSpec(memory_space=pltpu.VMEM)


def _rms(x, gain):
    return x * lax.rsqrt(jnp.mean(x * x, axis=-1, keepdims=True) + RMS_EPS) * gain


def _sigmoid(x):
    return 1.0 / (1.0 + jnp.exp(-x))


def _dot(a, b):
    return jnp.dot(a.astype(BF16), b.astype(BF16), preferred_element_type=F32)


def _dot_nt(a, b):
    return lax.dot_general(a.astype(BF16), b.astype(BF16), (((1,), (1,)), ((), ())),
                           preferred_element_type=F32)


def _dot_tn(a, b):
    return lax.dot_general(a.astype(BF16), b.astype(BF16), (((0,), (0,)), ((), ())),
                           preferred_element_type=F32)


def _headsum(x, bd_ref):
    d = x.shape[-1]
    bd = bd_ref[...]
    parts = [_dot(x[:, c:c + HEADSUM_W], bd) for c in range(0, d, HEADSUM_W)]
    return jnp.concatenate(parts, axis=-1)


def _block_diag_ones():
    i = np.arange(HEADSUM_W) // HEAD
    return jnp.asarray((i[:, None] == i[None, :]).astype(np.float32), dtype=BF16)


def _ffn_tile(x, g_ref, win_ref, wout_ref):
    d_ff = wout_ref.shape[0]
    h = _rms(x, g_ref[...]).astype(BF16)
    acc = x
    for c in range(0, d_ff, FFN_TILE):
        gate = jnp.dot(h, win_ref[:, c:c + FFN_TILE], preferred_element_type=F32)
        up = jnp.dot(h, win_ref[:, d_ff + c:d_ff + c + FFN_TILE], preferred_element_type=F32)
        act = (gate * _sigmoid(gate) * up).astype(BF16)
        acc = acc + jnp.dot(act, wout_ref[c:c + FFN_TILE, :], preferred_element_type=F32)
    return acc


def _rwkv_proj_kernel(*refs, has_vgate):
    if has_vgate:
        (x_ref, xp_ref, gain_ref, mu_ref, wrkv_ref, w0_ref, w1_ref, w2_ref, a0_ref, a1_ref, a2_ref,
         g1_ref, g2_ref, kk_ref, ka_ref, bd_ref, v0_ref, v1_ref, v2_ref, vf_ref,
         r_o, k_o, v_o, lw_o, kn_o, as_o, g_o) = refs
    else:
        (x_ref, xp_ref, gain_ref, mu_ref, wrkv_ref, w0_ref, w1_ref, w2_ref, a0_ref, a1_ref, a2_ref,
         g1_ref, g2_ref, kk_ref, ka_ref, bd_ref,
         r_o, k_o, v_o, lw_o, kn_o, as_o, g_o) = refs
    gain = gain_ref[...]
    h = _rms(x_ref[0], gain)
    hp_row = _rms(xp_ref[0], gain)[7:8, :]
    hp_row = jnp.where(pl.program_id(1) == 0, 0.0, hp_row)
    row = lax.broadcasted_iota(jnp.int32, h.shape, 0)
    h_prev = jnp.where(row == 0, hp_row, pltpu.roll(h, 1, axis=0))
    xx = h_prev - h

    def mix(c):
        return (h + xx * mu_ref[c:c + 1, :]).astype(BF16)

    xr, xk, xv, xw, xa, xg = (mix(c) for c in range(6))
    r = jnp.dot(xr, wrkv_ref[0], preferred_element_type=F32)
    k = jnp.dot(xk, wrkv_ref[1], preferred_element_type=F32)
    v = jnp.dot(xv, wrkv_ref[2], preferred_element_type=F32)

    wl = w0_ref[...] + _dot(jnp.tanh(jnp.dot(xw, w1_ref[...], preferred_element_type=F32)), w2_ref[...])
    w = -(jnp.maximum(-wl, 0.0) + jnp.log(1.0 + jnp.exp(-jnp.abs(wl)))) - 0.5
    lw_o[0] = -jnp.exp(w)
    a = _sigmoid(a0_ref[...] + _dot(jnp.dot(xa, a1_ref[...], preferred_element_type=F32), a2_ref[...]))
    g = _dot(_sigmoid(jnp.dot(xg, g1_ref[...], preferred_element_type=F32)), g2_ref[...])
    if has_vgate:
        gate = _sigmoid(v0_ref[...] + _dot(jnp.dot(xv, v1_ref[...], preferred_element_type=F32), v2_ref[...]))
        v = v + (vf_ref[0].astype(F32) - v) * gate

    kk = k * kk_ref[...]
    kn = kk * lax.rsqrt(jnp.maximum(_headsum(kk * kk, bd_ref), 1e-24))
    k = k * (1.0 + (a - 1.0) * ka_ref[...])
    r_o[0] = r.astype(BF16)
    k_o[0] = k.astype(BF16)
    v_o[0] = v.astype(BF16)
    kn_o[0] = kn.astype(BF16)
    as_o[0] = a.astype(BF16)
    g_o[0] = g.astype(BF16)


def _rwkv_proj(x, gain, mu, w_rkv, w0, w1, w2, a0, a1, a2, g1, g2, k_k, k_a, v_gate, v_first):
    B, S, D = x.shape
    tm = TOKEN_TILE
    has_vgate = v_gate is not None
    row = lambda t: t.reshape(1, D)
    tile = pl.BlockSpec((1, tm, D), lambda b, i: (b, i, 0))
    prev = pl.BlockSpec((1, 8, D), lambda b, i: (b, jnp.maximum(i * (tm // 8) - 1, 0), 0))
    vec = pl.BlockSpec((1, D), lambda b, i: (0, 0))
    args = [x, x, row(gain), mu, w_rkv.astype(BF16), row(w0), w1.astype(BF16), w2.astype(BF16),
            row(a0), a1.astype(BF16), a2.astype(BF16), g1.astype(BF16), g2.astype(BF16),
            row(k_k), row(k_a), _block_diag_ones()]
    specs = [tile, prev, vec, _vmem_spec(), _vmem_spec(), vec, _vmem_spec(), _vmem_spec(),
             vec, _vmem_spec(), _vmem_spec(), _vmem_spec(), _vmem_spec(), vec, vec, _vmem_spec()]
    if has_vgate:
        v0, v1, v2 = v_gate
        args += [row(v0), v1.astype(BF16), v2.astype(BF16), v_first]
        specs += [vec, _vmem_spec(), _vmem_spec(), tile]
    bshape = jax.ShapeDtypeStruct((B, S, D), BF16)
    out_shape = [bshape, bshape, bshape, jax.ShapeDtypeStruct((B, S, D), F32), bshape, bshape, bshape]
    return pl.pallas_call(
        functools.partial(_rwkv_proj_kernel, has_vgate=has_vgate),
        grid=(B, S // tm),
        in_specs=specs,
        out_specs=[tile] * 7,
        out_shape=out_shape,
        compiler_params=_cparams(("parallel", "parallel")),
        name="rwkv_proj",
    )(*args)


def _rwkv_scan_kernel(r_ref, k_ref, v_ref, lw_ref, kn_ref, as_ref, tri_ref, y_ref, state_ref, *,
                      n_pairs, n_rows):
    C = CHUNK

    @pl.when(pl.program_id(1) == 0)
    def _():
        state_ref[...] = jnp.zeros_like(state_ref)

    tri = tri_ref[...]
    r_d, a_d, k_d, b_d, k_e, b_e, w_tot, v = ([] for _ in range(8))
    for bi in range(n_rows):
        lw = lw_ref[bi]
        lw_hi = lw.astype(BF16)
        rem = lw - lw_hi.astype(F32)
        lw_mid = rem.astype(BF16)
        lw_lo = (rem - lw_mid.astype(F32)).astype(BF16)
        cum = (jnp.dot(tri, lw_hi, preferred_element_type=F32) + jnp.dot(tri, lw_mid, preferred_element_type=F32)
               + jnp.dot(tri, lw_lo, preferred_element_type=F32))
        tot = cum[C - 1:C, :]
        r = r_ref[bi].astype(F32)
        k = k_ref[bi].astype(F32)
        kn = kn_ref[bi].astype(F32)
        b = kn * as_ref[bi].astype(F32)
        w_inv = jnp.exp(-cum)
        w_end = jnp.exp(tot - cum)
        r_d.append((r * jnp.exp(cum)).astype(BF16))
        a_d.append((-kn * jnp.exp(cum - lw)).astype(BF16))
        k_d.append((k * w_inv).astype(BF16))
        b_d.append((b * w_inv).astype(BF16))
        k_e.append((k * w_end).astype(BF16))
        b_e.append((b * w_end).astype(BF16))
        w_tot.append(jnp.exp(tot))
        v.append(v_ref[bi])

    ri = lax.broadcasted_iota(jnp.int32, (C, PAIR), 0)
    lane = lax.broadcasted_iota(jnp.int32, (C, PAIR), 1)
    even = lane < HEAD
    ci = jnp.where(even, lane, lane - HEAD)
    strict = ri > ci
    incl = ri >= ci
    eye = jnp.where(ri == ci, 1.0, 0.0).astype(F32)

    def bdiag(yp):
        yp = yp.astype(BF16)
        zero = jnp.zeros_like(yp)
        return jnp.concatenate([jnp.where(even, yp, zero), jnp.where(even, zero, yp)], axis=0)

    ch = [(bi, j) for bi in range(n_rows) for j in range(n_pairs)]
    n = range(len(ch))
    sls = [slice(j * PAIR, (j + 1) * PAIR) for _, j in ch]
    pick = lambda arrs, c: arrs[ch[c][0]][:, sls[c]]
    xa_r = [jnp.concatenate([pick(a_d, c), pick(r_d, c)], axis=0) for c in n]
    g_k = [_dot_nt(xa_r[c], bdiag(pick(k_d, c))) for c in n]
    g_b = [_dot_nt(xa_r[c], bdiag(pick(b_d, c))) for c in n]
    a_ab = [jnp.where(strict, g_b[c][:C], 0.0) for c in n]
    a_kk = [jnp.concatenate([jnp.where(strict, g_k[c][:C], 0.0), jnp.where(incl, g_k[c][C:], 0.0)],
                            axis=0).astype(BF16) for c in n]
    a_rb = [jnp.where(incl, g_b[c][C:], 0.0).astype(BF16) for c in n]
    p = [eye + a_ab[c] for c in n]
    q = [_dot(a_ab[c], bdiag(a_ab[c])) for c in n]
    for _ in range(int(math.log2(C)) - 2):
        qp = [_dot(jnp.concatenate([q[c], p[c]], axis=0), bdiag(q[c])) for c in n]
        q = [qp[c][:C] for c in n]
        p = [p[c] + qp[c][C:] for c in n]
    p = [p[c] + _dot(p[c], bdiag(q[c])) for c in n]
    s0 = [state_ref[bi, j] for bi, j in ch]
    xs = [_dot_nt(xa_r[c], bdiag(s0[c])) for c in n]
    av = [_dot(a_kk[c], bdiag(pick(v, c))) for c in n]
    u = [_dot(p[c], bdiag(xs[c][:C] + av[c][:C])) for c in n]
    y = [xs[c][C:] + av[c][C:] + _dot(a_rb[c], bdiag(u[c])) for c in n]
    for c in n:
        bi, j = ch[c]
        y_ref[bi, :, sls[c]] = y[c].astype(y_ref.dtype)
        vu = jnp.concatenate([pick(v, c), u[c].astype(BF16)], axis=0)
        kb = jnp.concatenate([pick(k_e, c), pick(b_e, c)], axis=0)
        upd = _dot_tn(vu, kb)
        state_ref[bi, j] = s0[c] * pick(w_tot, c) + jnp.where(even, upd[:HEAD], upd[HEAD:])


def _rwkv_scan(r, k, v, lw, kn, a_sig):
    B, S, D = r.shape
    n_pairs = D // PAIR
    n_rows = math.gcd(B, SCAN_ROWS)
    tile = pl.BlockSpec((n_rows, CHUNK, D), lambda b, c: (b, c, 0))
    tri = jnp.asarray(np.tril(np.ones((CHUNK, CHUNK), np.float32)), dtype=BF16)
    return pl.pallas_call(
        functools.partial(_rwkv_scan_kernel, n_pairs=n_pairs, n_rows=n_rows),
        grid=(B // n_rows, S // CHUNK),
        in_specs=[tile] * 6 + [pl.BlockSpec((CHUNK, CHUNK), lambda b, c: (0, 0))],
        out_specs=tile,
        out_shape=jax.ShapeDtypeStruct((B, S, D), BF16),
        scratch_shapes=[pltpu.VMEM((n_rows, n_pairs, HEAD, PAIR), F32)],
        compiler_params=_cparams(("parallel", "arbitrary")),
        name="rwkv_scan",
    )(r, k, v, lw, kn, a_sig, tri)


def _rwkv_out_kernel(x_ref, y_ref, r_ref, k_ref, v_ref, g_ref, lnw_ref, lnb_ref, rk_ref, bd_ref, wo_ref,
                     fg_ref, win_ref, wout_ref, o_ref):
    y = y_ref[0].astype(F32)
    inv_n = 1.0 / HEAD
    mean = _headsum(y, bd_ref) * inv_n
    d = y - mean
    var = _headsum(d * d, bd_ref) * inv_n
    yn = d * lax.rsqrt(var + RWKV_LN_EPS) * lnw_ref[...] + lnb_ref[...]
    rk = r_ref[0].astype(F32) * k_ref[0].astype(F32) * rk_ref[...]
    bonus = _headsum(rk, bd_ref) * v_ref[0].astype(F32)
    z = ((yn + bonus) * g_ref[0].astype(F32)).astype(BF16)
    x2 = x_ref[0] + jnp.dot(z, wo_ref[...], preferred_element_type=F32)
    o_ref[0] = _ffn_tile(x2, fg_ref, win_ref, wout_ref)


def _rwkv_out(x, y, r, k, v, g, ln_w, ln_b, r_k, w_o, ffn_gain, w_in, w_out):
    B, S, D = x.shape
    tm = TOKEN_TILE
    tile = pl.BlockSpec((1, tm, D), lambda b, i: (b, i, 0))
    vec = pl.BlockSpec((1, D), lambda b, i: (0, 0))
    return pl.pallas_call(
        _rwkv_out_kernel,
        grid=(B, S // tm),
        in_specs=[tile] * 6 + [vec, vec, vec, _vmem_spec(), _vmem_spec(), vec, _vmem_spec(), _vmem_spec()],
        out_specs=tile,
        out_shape=jax.ShapeDtypeStruct((B, S, D), F32),
        compiler_params=_cparams(("parallel", "parallel")),
        name="rwkv_out_ffn",
    )(x, y, r, k, v, g, ln_w.reshape(1, D), ln_b.reshape(1, D), r_k.reshape(1, D),
      _block_diag_ones(), w_o.astype(BF16), ffn_gain.reshape(1, D), w_in.astype(BF16), w_out.astype(BF16))


def _attn_qkv_kernel(x_ref, gain_ref, w_ref, qg_ref, kg_ref, bd_ref, *refs, dm, dils):
    outs, h_scr = refs[:-1], refs[-1]
    tm = x_ref.shape[1]
    h = _rms(x_ref[0], gain_ref[...])
    n_col = h.shape[1] // LANES
    for c in range(n_col):
        h_scr[c] = h[:, c * LANES:(c + 1) * LANES]
    inv_n = 1.0 / HEAD
    for g, d in enumerate(dils):
        n = tm // d
        if d == 1:
            hp = h
        else:
            hp = jnp.concatenate(
                [jnp.concatenate([h_scr[c, pl.ds(r, n, stride=d), :] for c in range(n_col)], axis=1)
                 for r in range(d)], axis=0)
        qkv = jnp.dot(hp.astype(BF16), w_ref[:, g * 3 * dm:(g + 1) * 3 * dm], preferred_element_type=F32)
        q, k, v = qkv[:, :dm], qkv[:, dm:2 * dm], qkv[:, 2 * dm:]
        q = q * lax.rsqrt(_headsum(q * q, bd_ref) * inv_n + RMS_EPS) * qg_ref[g:g + 1, :]
        k = k * lax.rsqrt(_headsum(k * k, bd_ref) * inv_n + RMS_EPS) * kg_ref[g:g + 1, :]
        for t, o_ref in zip((q, k, v), outs[3 * g:3 * g + 3]):
            t = t.astype(BF16)
            for r in range(d):
                o_ref[0, r] = t[r * n:(r + 1) * n]


def _attn_qkv(x, gain, w, q_gain, k_gain):
    B, S, D = x.shape
    dm = w.shape[1] // (3 * len(DILATION_PAIRS))
    n_heads = dm // HEAD
    tm = TOKEN_TILE
    dils = tuple(d for _, d in DILATION_PAIRS)
    scale = HEAD ** -0.5 * LOG2E
    out_shape, out_specs = [], []
    for d in dils:
        for _ in range(3):
            out_shape.append(jax.ShapeDtypeStruct((B, d, S // d, dm), BF16))
            out_specs.append(pl.BlockSpec((1, d, tm // d, dm), lambda b, i: (b, 0, i, 0)))
    vec = lambda rows, n: pl.BlockSpec((rows, n), lambda b, i: (0, 0))
    return pl.pallas_call(
        functools.partial(_attn_qkv_kernel, dm=dm, dils=dils),
        grid=(B, S // tm),
        in_specs=[pl.BlockSpec((1, tm, D), lambda b, i: (b, i, 0)), vec(1, D), _vmem_spec(),
                  vec(len(dils), dm), vec(len(dils), dm), _vmem_spec()],
        out_specs=out_specs,
        out_shape=out_shape,
        scratch_shapes=[pltpu.VMEM((D // LANES, tm, LANES), F32)],
        compiler_params=_cparams(("parallel", "parallel")),
        name="attn_qkv",
    )(x, gain.reshape(1, D), w.astype(BF16), jnp.tile(q_gain, (1, n_heads)) * scale,
      jnp.tile(k_gain, (1, n_heads)), _block_diag_ones())


def _attn_core_kernel(q_ref, kp_ref, kc_ref, vp_ref, vc_ref, bias_ref, o_ref, lse_ref, *, n_heads):
    q = q_ref[0, 0]
    kp, kc, vp, vc = kp_ref[0, 0], kc_ref[0, 0], vp_ref[0, 0], vc_ref[0, 0]
    tbl = jnp.minimum(pl.program_id(2), 1)
    lane_q = lax.broadcasted_iota(jnp.int32, (BAND, PAIR), 1)
    even = lane_q < HEAD
    lane = lax.broadcasted_iota(jnp.int32, (BAND, LSE_W), 1)
    m_pad = jnp.zeros((BAND, LSE_W), F32)
    l_pad = jnp.ones((BAND, LSE_W), F32)
    zero = jnp.zeros((BAND, PAIR), BF16)

    def scores(pairs):
        logits = {}
        for j in pairs:
            sl = slice(j * PAIR, (j + 1) * PAIR)
            keys = jnp.concatenate([kp[:, sl], kc[:, sl]], axis=0)
            qp = q[:, sl]
            for hd, qm in ((2 * j, jnp.where(even, qp, zero)), (2 * j + 1, jnp.where(even, zero, qp))):
                s = lax.dot_general(qm, keys, (((1,), (1,)), ((), ())), preferred_element_type=F32)
                logits[hd] = s + bias_ref[tbl, hd]
        return logits

    def softmax_parts(logits):
        parts = {}
        for hd, s in logits.items():
            m = jnp.max(s, axis=-1, keepdims=True)
            e = jnp.exp2(s - m)
            parts[hd] = (m, jnp.sum(e, axis=-1, keepdims=True), e.astype(BF16))
        return parts

    def weighted_values(pairs, parts, m_pad, l_pad):
        for j in pairs:
            sl = slice(j * PAIR, (j + 1) * PAIR)
            vals = jnp.concatenate([vp[:, sl], vc[:, sl]], axis=0)
            (m_e, l_e, p_e), (m_o, l_o, p_o) = parts[2 * j], parts[2 * j + 1]
            o_e = jnp.dot(p_e, vals, preferred_element_type=F32) / l_e
            o_o = jnp.dot(p_o, vals, preferred_element_type=F32) / l_o
            o_ref[0, 0, :, sl] = jnp.where(even, o_e, o_o).astype(o_ref.dtype)
            for hd, m, l in ((2 * j, m_e, l_e), (2 * j + 1, m_o, l_o)):
                m_pad = jnp.where(lane == hd, m, m_pad)
                l_pad = jnp.where(lane == hd, l, l_pad)
        return m_pad, l_pad

    n_pairs = n_heads // 2
    groups = [range(j0, min(j0 + CORE_GROUP, n_pairs)) for j0 in range(0, n_pairs, CORE_GROUP)]
    logits_q, parts_q = [], []
    for step in range(len(groups) + 2):
        if step < len(groups):
            logits_q.append(scores(groups[step]))
        if 1 <= step <= len(groups):
            parts_q.append(softmax_parts(logits_q[step - 1]))
        if step >= 2:
            m_pad, l_pad = weighted_values(groups[step - 2], parts_q[step - 2], m_pad, l_pad)
    lse_ref[0, 0] = (m_pad + jnp.log2(l_pad)) * LN2


def _attn_core(q, k, v, bias, dilation):
    B, d, sd, dm = q.shape
    n_heads = dm // HEAD
    nb = sd // BAND
    cur = pl.BlockSpec((1, 1, BAND, dm), lambda b, r, n: (b, r, n, 0))
    prv = pl.BlockSpec((1, 1, BAND, dm), lambda b, r, n: (b, r, jnp.maximum(n - 1, 0), 0))
    return pl.pallas_call(
        functools.partial(_attn_core_kernel, n_heads=n_heads),
        grid=(B, d, nb),
        in_specs=[cur, prv, cur, prv, cur, _vmem_spec()],
        out_specs=[cur, pl.BlockSpec((1, 1, BAND, LSE_W), lambda b, r, n: (b, r, n, 0))],
        out_shape=[jax.ShapeDtypeStruct((B, d, sd, dm), BF16),
                   jax.ShapeDtypeStruct((B, d, sd, LSE_W), F32)],
        compiler_params=_cparams(("parallel", "parallel", "parallel")),
        name=f"attn_core_d{dilation}",
    )(q, k, k, v, v, bias)


def _attn_out_kernel(x_ref, o1_ref, o2_ref, o3_ref, l1_ref, l2_ref, l3_ref, ex_ref, wo_ref,
                     fg_ref, win_ref, wout_ref, out_ref, o_scr, l_scr):
    tm = x_ref.shape[1]

    def token_order(ref, scr):
        d = ref.shape[1]
        if d == 1:
            return ref[0, 0].astype(F32)
        n_col = ref.shape[3] // LANES
        for r in range(d):
            blk = ref[0, r].astype(F32)
            for c in range(n_col):
                scr[c, pl.ds(r, tm // d, stride=d), :] = blk[:, c * LANES:(c + 1) * LANES]
        return jnp.concatenate([scr[c] for c in range(n_col)], axis=1)

    ls = [token_order(l_ref, l_scr.at[pl.ds(i, 1)]) for i, l_ref in enumerate((l1_ref, l2_ref, l3_ref))]
    m = jnp.maximum(jnp.maximum(ls[0], ls[1]), ls[2])
    es = [jnp.exp(l - m) for l in ls]
    inv = 1.0 / (es[0] + es[1] + es[2])
    def head_lanes(wgt):
        hi = wgt.astype(BF16)
        lo = (wgt - hi.astype(F32)).astype(BF16)
        return (jnp.dot(hi, ex_ref[...], preferred_element_type=F32)
                + jnp.dot(lo, ex_ref[...], preferred_element_type=F32))

    w1 = head_lanes(es[0] * inv)
    w2 = head_lanes(es[1] * inv)
    acc = w1 * token_order(o1_ref, o_scr)
    acc = acc + w2 * token_order(o2_ref, o_scr)
    acc = acc + (1.0 - w1 - w2) * token_order(o3_ref, o_scr)
    x2 = x_ref[0] + jnp.dot(acc.astype(BF16), wo_ref[...], preferred_element_type=F32)
    out_ref[0] = _ffn_tile(x2, fg_ref, win_ref, wout_ref)


def _attn_out(x, outs, lses, w_o, ffn_gain, w_in, w_out):
    B, S, D = x.shape
    dm = w_o.shape[0]
    n_heads = dm // HEAD
    tm = TOKEN_TILE
    expand = np.zeros((LSE_W, dm), np.float32)
    for hd in range(n_heads):
        expand[hd, hd * HEAD:(hd + 1) * HEAD] = 1.0
    tile = pl.BlockSpec((1, tm, D), lambda b, i: (b, i, 0))
    vec = pl.BlockSpec((1, D), lambda b, i: (0, 0))
    grouped = lambda t: pl.BlockSpec((1, t.shape[1], tm // t.shape[1], t.shape[3]), lambda b, i: (b, 0, i, 0))
    return pl.pallas_call(
        _attn_out_kernel,
        grid=(B, S // tm),
        in_specs=[tile] + [grouped(t) for t in outs] + [grouped(t) for t in lses]
                 + [_vmem_spec(), _vmem_spec(), vec, _vmem_spec(), _vmem_spec()],
        out_specs=tile,
        out_shape=jax.ShapeDtypeStruct((B, S, D), F32),
        scratch_shapes=[pltpu.VMEM((dm // LANES, tm, LANES), F32), pltpu.VMEM((len(lses), tm, LSE_W), F32)],
        compiler_params=_cparams(("parallel", "parallel")),
        name="attn_out_ffn",
    )(x, *outs, *lses, jnp.asarray(expand, dtype=BF16), w_o.astype(BF16),
      ffn_gain.reshape(1, D), w_in.astype(BF16), w_out.astype(BF16))


def _t5_bucket(dist):
    max_exact = NUM_BUCKETS // 2
    large = max_exact + (jnp.log(jnp.maximum(dist, 1).astype(F32) / max_exact)
                         / math.log(MAX_DISTANCE / max_exact) * (NUM_BUCKETS - max_exact)).astype(jnp.int32)
    large = jnp.minimum(large, NUM_BUCKETS - 1)
    return jnp.where(dist < max_exact, dist, large)


def _band_bias(table_g, window, dilation):
    steps = jnp.arange(3 * BAND - 1) - (BAND - 1)
    allowed = (steps >= 0) & (steps <= window // dilation)
    bucket = _t5_bucket(jnp.maximum(steps, 0) * dilation)
    by_step = jnp.where(allowed[:, None], jnp.take(table_g, bucket, axis=0).astype(F32) * LOG2E, NEG_INF)
    rev = by_step[::-1].T
    bias = jnp.stack([rev[:, BAND - 1 - i:3 * BAND - 1 - i] for i in range(BAND)], axis=1)
    first = jnp.where(jnp.arange(2 * BAND) >= BAND, bias, NEG_INF)
    return jnp.stack([first, bias])


def _attn_mixer(x, gain, w_qkv, q_gain, k_gain, w_o, rel_bias, ffn_gain, w_in, w_out):
    dm = w_o.shape[0]
    n_heads = dm // HEAD
    qkv = _attn_qkv(x, gain, w_qkv, q_gain, k_gain)
    outs, lses = [], []
    for g, (window, dilation) in enumerate(DILATION_PAIRS):
        q, k, v = qkv[3 * g:3 * g + 3]
        bias = _band_bias(rel_bias[:, g * n_heads:(g + 1) * n_heads], window, dilation)
        o, lse = _attn_core(q, k, v, bias, dilation)
        outs.append(o)
        lses.append(lse)
    return _attn_out(x, outs, lses, w_o, ffn_gain, w_in, w_out)


def kernel(x, mix_norm, ffn_norm, ffn_w_in, ffn_w_out, rwkv_mu, rwkv_w_rkv, rwkv_w0, rwkv_w1, rwkv_w2, rwkv_a0, rwkv_a1, rwkv_a2, rwkv_v0, rwkv_v1, rwkv_v2, rwkv_g1, rwkv_g2, rwkv_k_k, rwkv_k_a, rwkv_r_k, rwkv_ln_w, rwkv_ln_b, rwkv_w_o, attn_w_qkv, attn_q_gain, attn_k_gain, attn_w_o, rel_bias):
    depth = mix_norm.shape[0]
    v_first = None
    for layer in range(depth):
        i = layer // 2
        ffn = (ffn_norm[layer], ffn_w_in[layer], ffn_w_out[layer])
        if layer % 2 == 0:
            v_gate = None if i == 0 else (rwkv_v0[i - 1], rwkv_v1[i - 1], rwkv_v2[i - 1])
            r, k, v, lw, kn, a_sig, g = _rwkv_proj(
                x, mix_norm[layer], rwkv_mu[i], rwkv_w_rkv[i], rwkv_w0[i], rwkv_w1[i], rwkv_w2[i],
                rwkv_a0[i], rwkv_a1[i], rwkv_a2[i], rwkv_g1[i], rwkv_g2[i], rwkv_k_k[i], rwkv_k_a[i],
                v_gate, v_first)
            if v_first is None:
                v_first = v
            y = _rwkv_scan(r, k, v, lw, kn, a_sig)
            x = _rwkv_out(x, y, r, k, v, g, rwkv_ln_w[i], rwkv_ln_b[i], rwkv_r_k[i], rwkv_w_o[i], *ffn)
        else:
            x = _attn_mixer(x, mix_norm[layer], attn_w_qkv[i], attn_q_gain[i], attn_k_gain[i],
                            attn_w_o[i], rel_bias, *ffn)
    return x
```

```python
import functools
import math

import numpy as np
import jax
import jax.numpy as jnp
from jax import lax
from jax.experimental import pallas as pl
from jax.experimental.pallas import tpu as pltpu

F32 = jnp.float32
BF16 = jnp.bfloat16

HEAD = 64
LANES = 128
PAIR = 2 * HEAD
RMS_EPS = 1e-6
RWKV_LN_EPS = 64e-5
DILATION_PAIRS = ((128, 1), (512, 4), (2048, 16))
BAND = 128
NUM_BUCKETS = 32
MAX_DISTANCE = 2048
NEG_INF = -1e30
CHUNK = 64
SCAN_ROWS = 4
HEADSUM_W = 256
LSE_W = 128
CORE_GROUP = 2
CORE_BLOCKS = 2
LOG2E = math.log2(math.e)
LN2 = math.log(2.0)
TOKEN_TILE = 512
FFN_TILE = 256
VMEM_LIMIT = 56 * 1024 * 1024


def _cparams(sem):
    return pltpu.CompilerParams(dimension_semantics=sem, vmem_limit_bytes=VMEM_LIMIT)


def _vmem_spec():
    return pl.BlockSpec(memory_space=pltpu.VMEM)


def _rms(x, gain):
    return x * lax.rsqrt(jnp.mean(x * x, axis=-1, keepdims=True) + RMS_EPS) * gain


def _sigmoid(x):
    return 1.0 / (1.0 + jnp.exp(-x))


def _dot(a, b):
    return jnp.dot(a.astype(BF16), b.astype(BF16), preferred_element_type=F32)


def _dot_nt(a, b):
    return lax.dot_general(a.astype(BF16), b.astype(BF16), (((1,), (1,)), ((), ())),
                           preferred_element_type=F32)


def _dot_tn(a, b):
    return lax.dot_general(a.astype(BF16), b.astype(BF16), (((0,), (0,)), ((), ())),
                           preferred_element_type=F32)


def _headsum(x, bd_ref):
    d = x.shape[-1]
    bd = bd_ref[...]
    parts = [_dot(x[:, c:c + HEADSUM_W], bd) for c in range(0, d, HEADSUM_W)]
    return jnp.concatenate(parts, axis=-1)


def _block_diag_ones():
    i = np.arange(HEADSUM_W) // HEAD
    return jnp.asarray((i[:, None] == i[None, :]).astype(np.float32), dtype=BF16)


def _ffn_tile(x, g_ref, win_ref, wout_ref):
    d_ff = wout_ref.shape[0]
    h = _rms(x, g_ref[...]).astype(BF16)
    acc = x
    for c in range(0, d_ff, FFN_TILE):
        gate = jnp.dot(h, win_ref[:, c:c + FFN_TILE], preferred_element_type=F32)
        up = jnp.dot(h, win_ref[:, d_ff + c:d_ff + c + FFN_TILE], preferred_element_type=F32)
        act = (gate * _sigmoid(gate) * up).astype(BF16)
        acc = acc + jnp.dot(act, wout_ref[c:c + FFN_TILE, :], preferred_element_type=F32)
    return acc


def _rwkv_proj_kernel(*refs, has_vgate):
    if has_vgate:
        (x_ref, xp_ref, gain_ref, mu_ref, wrkv_ref, w0_ref, w1_ref, w2_ref, a0_ref, a1_ref, a2_ref,
         g1_ref, g2_ref, kk_ref, ka_ref, bd_ref, v0_ref, v1_ref, v2_ref, vf_ref,
         r_o, k_o, v_o, lw_o, kn_o, as_o, g_o) = refs
    else:
        (x_ref, xp_ref, gain_ref, mu_ref, wrkv_ref, w0_ref, w1_ref, w2_ref, a0_ref, a1_ref, a2_ref,
         g1_ref, g2_ref, kk_ref, ka_ref, bd_ref,
         r_o, k_o, v_o, lw_o, kn_o, as_o, g_o) = refs
    gain = gain_ref[...]
    h = _rms(x_ref[0], gain)
    hp_row = _rms(xp_ref[0], gain)[7:8, :]
    hp_row = jnp.where(pl.program_id(1) == 0, 0.0, hp_row)
    row = lax.broadcasted_iota(jnp.int32, h.shape, 0)
    h_prev = jnp.where(row == 0, hp_row, pltpu.roll(h, 1, axis=0))
    xx = h_prev - h

    def mix(c):
        return (h + xx * mu_ref[c:c + 1, :]).astype(BF16)

    xw, xa, xg, xv = mix(3), mix(4), mix(5), mix(2)
    t_w = jnp.dot(xw, w1_ref[...], preferred_element_type=F32)
    t_a = jnp.dot(xa, a1_ref[...], preferred_element_type=F32)
    t_g = jnp.dot(xg, g1_ref[...], preferred_element_type=F32)
    if has_vgate:
        t_v = jnp.dot(xv, v1_ref[...], preferred_element_type=F32)
    wl = w0_ref[...] + _dot(jnp.tanh(t_w), w2_ref[...])
    al = a0_ref[...] + _dot(t_a, a2_ref[...])
    g = _dot(_sigmoid(t_g), g2_ref[...])
    if has_vgate:
        vl = v0_ref[...] + _dot(t_v, v2_ref[...])
    xr, xk = mix(0), mix(1)
    r = jnp.dot(xr, wrkv_ref[0], preferred_element_type=F32)
    k = jnp.dot(xk, wrkv_ref[1], preferred_element_type=F32)
    v = jnp.dot(xv, wrkv_ref[2], preferred_element_type=F32)
    w = -(jnp.maximum(-wl, 0.0) + jnp.log(1.0 + jnp.exp(-jnp.abs(wl)))) - 0.5
    lw_o[0] = -jnp.exp(w)
    a = _sigmoid(al)
    if has_vgate:
        v = v + (vf_ref[0].astype(F32) - v) * _sigmoid(vl)

    kk = k * kk_ref[...]
    kn = kk * lax.rsqrt(jnp.maximum(_headsum(kk * kk, bd_ref), 1e-24))
    k = k * (1.0 + (a - 1.0) * ka_ref[...])
    r_o[0] = r.astype(BF16)
    k_o[0] = k.astype(BF16)
    v_o[0] = v.astype(BF16)
    kn_o[0] = kn.astype(BF16)
    as_o[0] = a.astype(BF16)
    g_o[0] = g.astype(BF16)


def _rwkv_proj(x, gain, mu, w_rkv, w0, w1, w2, a0, a1, a2, g1, g2, k_k, k_a, v_gate, v_first):
    B, S, D = x.shape
    tm = TOKEN_TILE
    has_vgate = v_gate is not None
    row = lambda t: t.reshape(1, D)
    tile = pl.BlockSpec((1, tm, D), lambda b, i: (b, i, 0))
    prev = pl.BlockSpec((1, 8, D), lambda b, i: (b, jnp.maximum(i * (tm // 8) - 1, 0), 0))
    vec = pl.BlockSpec((1, D), lambda b, i: (0, 0))
    args = [x, x, row(gain), mu, w_rkv.astype(BF16), row(w0), w1.astype(BF16), w2.astype(BF16),
            row(a0), a1.astype(BF16), a2.astype(BF16), g1.astype(BF16), g2.astype(BF16),
            row(k_k), row(k_a), _block_diag_ones()]
    specs = [tile, prev, vec, _vmem_spec(), _vmem_spec(), vec, _vmem_spec(), _vmem_spec(),
             vec, _vmem_spec(), _vmem_spec(), _vmem_spec(), _vmem_spec(), vec, vec, _vmem_spec()]
    if has_vgate:
        v0, v1, v2 = v_gate
        args += [row(v0), v1.astype(BF16), v2.astype(BF16), v_first]
        specs += [vec, _vmem_spec(), _vmem_spec(), tile]
    bshape = jax.ShapeDtypeStruct((B, S, D), BF16)
    out_shape = [bshape, bshape, bshape, jax.ShapeDtypeStruct((B, S, D), F32), bshape, bshape, bshape]
    return pl.pallas_call(
        functools.partial(_rwkv_proj_kernel, has_vgate=has_vgate),
        grid=(B, S // tm),
        in_specs=specs,
        out_specs=[tile] * 7,
        out_shape=out_shape,
        compiler_params=_cparams(("parallel", "parallel")),
        name="rwkv_proj",
    )(*args)


def _rwkv_scan_kernel(r_ref, k_ref, v_ref, lw_ref, kn_ref, as_ref, tri_ref, y_ref, state_ref, *,
                      n_pairs, n_rows):
    C = CHUNK

    @pl.when(pl.program_id(1) == 0)
    def _():
        state_ref[...] = jnp.zeros_like(state_ref)

    tri = tri_ref[...]
    r_d, a_d, k_d, b_d, k_e, b_e, w_tot, v = ([] for _ in range(8))
    for bi in range(n_rows):
        lw = lw_ref[bi]
        lw_hi = lw.astype(BF16)
        rem = lw - lw_hi.astype(F32)
        lw_mid = rem.astype(BF16)
        lw_lo = (rem - lw_mid.astype(F32)).astype(BF16)
        cum = (jnp.dot(tri, lw_hi, preferred_element_type=F32) + jnp.dot(tri, lw_mid, preferred_element_type=F32)
               + jnp.dot(tri, lw_lo, preferred_element_type=F32))
        tot = cum[C - 1:C, :]
        r = r_ref[bi].astype(F32)
        k = k_ref[bi].astype(F32)
        kn = kn_ref[bi].astype(F32)
        b = kn * as_ref[bi].astype(F32)
        w_inv = jnp.exp(-cum)
        w_end = jnp.exp(tot - cum)
        r_d.append((r * jnp.exp(cum)).astype(BF16))
        a_d.append((-kn * jnp.exp(cum - lw)).astype(BF16))
        k_d.append((k * w_inv).astype(BF16))
        b_d.append((b * w_inv).astype(BF16))
        k_e.append((k * w_end).astype(BF16))
        b_e.append((b * w_end).astype(BF16))
        w_tot.append(jnp.exp(tot))
        v.append(v_ref[bi])

    ri = lax.broadcasted_iota(jnp.int32, (C, PAIR), 0)
    lane = lax.broadcasted_iota(jnp.int32, (C, PAIR), 1)
    even = lane < HEAD
    ci = jnp.where(even, lane, lane - HEAD)
    strict = ri > ci
    incl = ri >= ci
    eye = jnp.where(ri == ci, 1.0, 0.0).astype(F32)

    def bdiag(yp):
        yp = yp.astype(BF16)
        zero = jnp.zeros_like(yp)
        return jnp.concatenate([jnp.where(even, yp, zero), jnp.where(even, zero, yp)], axis=0)

    ch = [(bi, j) for bi in range(n_rows) for j in range(n_pairs)]
    n = range(len(ch))
    sls = [slice(j * PAIR, (j + 1) * PAIR) for _, j in ch]
    pick = lambda arrs, c: arrs[ch[c][0]][:, sls[c]]
    xa_r = [jnp.concatenate([pick(a_d, c), pick(r_d, c)], axis=0) for c in n]
    g = [_dot_nt(xa_r[c], jnp.concatenate([bdiag(pick(k_d, c)), bdiag(pick(b_d, c))], axis=0)) for c in n]
    a_ab = [jnp.where(strict, g[c][:C, PAIR:], 0.0) for c in n]
    a_kk = [jnp.concatenate([jnp.where(strict, g[c][:C, :PAIR], 0.0), jnp.where(incl, g[c][C:, :PAIR], 0.0)],
                            axis=0).astype(BF16) for c in n]
    a_rb = [jnp.where(incl, g[c][C:, PAIR:], 0.0).astype(BF16) for c in n]
    p = [eye + a_ab[c] for c in n]
    q = [_dot(a_ab[c], bdiag(a_ab[c])) for c in n]
    for _ in range(int(math.log2(C)) - 2):
        qp = [_dot(jnp.concatenate([q[c], p[c]], axis=0), bdiag(q[c])) for c in n]
        q = [qp[c][:C] for c in n]
        p = [p[c] + qp[c][C:] for c in n]
    p = [p[c] + _dot(p[c], bdiag(q[c])) for c in n]
    s0 = [state_ref[bi, j] for bi, j in ch]
    xs = [_dot_nt(xa_r[c], bdiag(s0[c])) for c in n]
    av = [_dot(a_kk[c], bdiag(pick(v, c))) for c in n]
    u = [_dot(p[c], bdiag(xs[c][:C] + av[c][:C])) for c in n]
    y = [xs[c][C:] + av[c][C:] + _dot(a_rb[c], bdiag(u[c])) for c in n]
    for c in n:
        bi, j = ch[c]
        y_ref[bi, :, sls[c]] = y[c].astype(y_ref.dtype)
        vu = jnp.concatenate([pick(v, c), u[c].astype(BF16)], axis=0)
        kb = jnp.concatenate([pick(k_e, c), pick(b_e, c)], axis=0)
        upd = _dot_tn(vu, kb)
        state_ref[bi, j] = s0[c] * pick(w_tot, c) + jnp.where(even, upd[:HEAD], upd[HEAD:])


def _rwkv_scan(r, k, v, lw, kn, a_sig):
    B, S, D = r.shape
    n_pairs = D // PAIR
    n_rows = math.gcd(B, SCAN_ROWS)
    tile = pl.BlockSpec((n_rows, CHUNK, D), lambda b, c: (b, c, 0))
    tri = jnp.asarray(np.tril(np.ones((CHUNK, CHUNK), np.float32)), dtype=BF16)
    return pl.pallas_call(
        functools.partial(_rwkv_scan_kernel, n_pairs=n_pairs, n_rows=n_rows),
        grid=(B // n_rows, S // CHUNK),
        in_specs=[tile] * 6 + [pl.BlockSpec((CHUNK, CHUNK), lambda b, c: (0, 0))],
        out_specs=tile,
        out_shape=jax.ShapeDtypeStruct((B, S, D), BF16),
        scratch_shapes=[pltpu.VMEM((n_rows, n_pairs, HEAD, PAIR), F32)],
        compiler_params=_cparams(("parallel", "arbitrary")),
        name="rwkv_scan",
    )(r, k, v, lw, kn, a_sig, tri)


def _rwkv_out_kernel(x_ref, y_ref, r_ref, k_ref, v_ref, g_ref, lnw_ref, lnb_ref, rk_ref, bd_ref, wo_ref,
                     fg_ref, win_ref, wout_ref, o_ref):
    y = y_ref[0].astype(F32)
    inv_n = 1.0 / HEAD
    mean = _headsum(y, bd_ref) * inv_n
    d = y - mean
    var = _headsum(d * d, bd_ref) * inv_n
    yn = d * lax.rsqrt(var + RWKV_LN_EPS) * lnw_ref[...] + lnb_ref[...]
    rk = r_ref[0].astype(F32) * k_ref[0].astype(F32) * rk_ref[...]
    bonus = _headsum(rk, bd_ref) * v_ref[0].astype(F32)
    z = ((yn + bonus) * g_ref[0].astype(F32)).astype(BF16)
    x2 = x_ref[0] + jnp.dot(z, wo_ref[...], preferred_element_type=F32)
    o_ref[0] = _ffn_tile(x2, fg_ref, win_ref, wout_ref)


def _rwkv_out(x, y, r, k, v, g, ln_w, ln_b, r_k, w_o, ffn_gain, w_in, w_out):
    B, S, D = x.shape
    tm = TOKEN_TILE
    tile = pl.BlockSpec((1, tm, D), lambda b, i: (b, i, 0))
    vec = pl.BlockSpec((1, D), lambda b, i: (0, 0))
    return pl.pallas_call(
        _rwkv_out_kernel,
        grid=(B, S // tm),
        in_specs=[tile] * 6 + [vec, vec, vec, _vmem_spec(), _vmem_spec(), vec, _vmem_spec(), _vmem_spec()],
        out_specs=tile,
        out_shape=jax.ShapeDtypeStruct((B, S, D), F32),
        compiler_params=_cparams(("parallel", "parallel")),
        name="rwkv_out_ffn",
    )(x, y, r, k, v, g, ln_w.reshape(1, D), ln_b.reshape(1, D), r_k.reshape(1, D),
      _block_diag_ones(), w_o.astype(BF16), ffn_gain.reshape(1, D), w_in.astype(BF16), w_out.astype(BF16))


def _attn_qkv_kernel(x_ref, gain_ref, w_ref, qg_ref, kg_ref, bd_ref, *refs, dm, dils):
    outs, h_scr = refs[:-1], refs[-1]
    tm = x_ref.shape[1]
    h = _rms(x_ref[0], gain_ref[...])
    n_col = h.shape[1] // LANES
    for c in range(n_col):
        h_scr[c] = h[:, c * LANES:(c + 1) * LANES]
    inv_n = 1.0 / HEAD
    for g, d in enumerate(dils):
        n = tm // d
        if d == 1:
            hp = h
        else:
            hp = jnp.concatenate(
                [jnp.concatenate([h_scr[c, pl.ds(r, n, stride=d), :] for c in range(n_col)], axis=1)
                 for r in range(d)], axis=0)
        qkv = jnp.dot(hp.astype(BF16), w_ref[:, g * 3 * dm:(g + 1) * 3 * dm], preferred_element_type=F32)
        q, k, v = qkv[:, :dm], qkv[:, dm:2 * dm], qkv[:, 2 * dm:]
        q = q * lax.rsqrt(_headsum(q * q, bd_ref) * inv_n + RMS_EPS) * qg_ref[g:g + 1, :]
        k = k * lax.rsqrt(_headsum(k * k, bd_ref) * inv_n + RMS_EPS) * kg_ref[g:g + 1, :]
        for t, o_ref in zip((q, k, v), outs[3 * g:3 * g + 3]):
            t = t.astype(BF16)
            for r in range(d):
                o_ref[0, r] = t[r * n:(r + 1) * n]


def _attn_qkv(x, gain, w, q_gain, k_gain):
    B, S, D = x.shape
    dm = w.shape[1] // (3 * len(DILATION_PAIRS))
    n_heads = dm // HEAD
    tm = TOKEN_TILE
    dils = tuple(d for _, d in DILATION_PAIRS)
    scale = HEAD ** -0.5 * LOG2E
    out_shape, out_specs = [], []
    for d in dils:
        for _ in range(3):
            out_shape.append(jax.ShapeDtypeStruct((B, d, S // d, dm), BF16))
            out_specs.append(pl.BlockSpec((1, d, tm // d, dm), lambda b, i: (b, 0, i, 0)))
    vec = lambda rows, n: pl.BlockSpec((rows, n), lambda b, i: (0, 0))
    return pl.pallas_call(
        functools.partial(_attn_qkv_kernel, dm=dm, dils=dils),
        grid=(B, S // tm),
        in_specs=[pl.BlockSpec((1, tm, D), lambda b, i: (b, i, 0)), vec(1, D), _vmem_spec(),
                  vec(len(dils), dm), vec(len(dils), dm), _vmem_spec()],
        out_specs=out_specs,
        out_shape=out_shape,
        scratch_shapes=[pltpu.VMEM((D // LANES, tm, LANES), F32)],
        compiler_params=_cparams(("parallel", "parallel")),
        name="attn_qkv",
    )(x, gain.reshape(1, D), w.astype(BF16), jnp.tile(q_gain, (1, n_heads)) * scale,
      jnp.tile(k_gain, (1, n_heads)), _block_diag_ones())


def _attn_core_kernel(q_ref, kp_ref, kc_ref, vp_ref, vc_ref, bias_ref, o_ref, lse_ref, *, n_heads, n_blk):
    q = q_ref[0, 0]
    k_all = jnp.concatenate([kp_ref[0, 0], kc_ref[0, 0]], axis=0)
    v_all = jnp.concatenate([vp_ref[0, 0], vc_ref[0, 0]], axis=0)
    first_tbl = jnp.minimum(pl.program_id(2), 1)
    lane_q = lax.broadcasted_iota(jnp.int32, (BAND, PAIR), 1)
    even = lane_q < HEAD
    lane = lax.broadcasted_iota(jnp.int32, (BAND, LSE_W), 1)
    zero = jnp.zeros((BAND, PAIR), BF16)

    def scores(t, pairs):
        tbl = first_tbl if t == 0 else 1
        logits = {}
        for j in pairs:
            sl = slice(j * PAIR, (j + 1) * PAIR)
            keys = k_all[t * BAND:(t + 2) * BAND, sl]
            qp = q[t * BAND:(t + 1) * BAND, sl]
            for hd, qm in ((2 * j, jnp.where(even, qp, zero)), (2 * j + 1, jnp.where(even, zero, qp))):
                s = lax.dot_general(qm, keys, (((1,), (1,)), ((), ())), preferred_element_type=F32)
                logits[hd] = s + bias_ref[tbl, hd]
        return logits

    def softmax_parts(logits):
        parts = {}
        for hd, s in logits.items():
            m = jnp.max(s, axis=-1, keepdims=True)
            e = jnp.exp2(s - m)
            parts[hd] = (m, jnp.sum(e, axis=-1, keepdims=True), e.astype(BF16))
        return parts

    def weighted_values(t, pairs, parts, m_pad, l_pad):
        for j in pairs:
            sl = slice(j * PAIR, (j + 1) * PAIR)
            vals = v_all[t * BAND:(t + 2) * BAND, sl]
            (m_e, l_e, p_e), (m_o, l_o, p_o) = parts[2 * j], parts[2 * j + 1]
            o_e = jnp.dot(p_e, vals, preferred_element_type=F32) / l_e
            o_o = jnp.dot(p_o, vals, preferred_element_type=F32) / l_o
            o_ref[0, 0, t * BAND:(t + 1) * BAND, sl] = jnp.where(even, o_e, o_o).astype(o_ref.dtype)
            for hd, m, l in ((2 * j, m_e, l_e), (2 * j + 1, m_o, l_o)):
                m_pad = jnp.where(lane == hd, m, m_pad)
                l_pad = jnp.where(lane == hd, l, l_pad)
        return m_pad, l_pad

    n_pairs = n_heads // 2
    units = [(t, range(j0, min(j0 + CORE_GROUP, n_pairs)))
             for t in range(n_blk) for j0 in range(0, n_pairs, CORE_GROUP)]
    m_pad = [jnp.zeros((BAND, LSE_W), F32)] * n_blk
    l_pad = [jnp.ones((BAND, LSE_W), F32)] * n_blk
    logits_q, parts_q = [], []
    for step in range(len(units) + 2):
        if step < len(units):
            logits_q.append(scores(*units[step]))
        if 1 <= step <= len(units):
            parts_q.append(softmax_parts(logits_q[step - 1]))
        if step >= 2:
            t, pairs = units[step - 2]
            m_pad[t], l_pad[t] = weighted_values(t, pairs, parts_q[step - 2], m_pad[t], l_pad[t])
    for t in range(n_blk):
        lse_ref[0, 0, t * BAND:(t + 1) * BAND] = (m_pad[t] + jnp.log2(l_pad[t])) * LN2


def _attn_core(q, k, v, bias, dilation):
    B, d, sd, dm = q.shape
    n_heads = dm // HEAD
    n_blk = math.gcd(sd // BAND, CORE_BLOCKS)
    rows = n_blk * BAND
    cur = lambda w: pl.BlockSpec((1, 1, rows, w), lambda b, r, n: (b, r, n, 0))
    prv = pl.BlockSpec((1, 1, BAND, dm), lambda b, r, n: (b, r, jnp.maximum(n * n_blk - 1, 0), 0))
    return pl.pallas_call(
        functools.partial(_attn_core_kernel, n_heads=n_heads, n_blk=n_blk),
        grid=(B, d, sd // rows),
        in_specs=[cur(dm), prv, cur(dm), prv, cur(dm), _vmem_spec()],
        out_specs=[cur(dm), cur(LSE_W)],
        out_shape=[jax.ShapeDtypeStruct((B, d, sd, dm), BF16),
                   jax.ShapeDtypeStruct((B, d, sd, LSE_W), F32)],
        compiler_params=_cparams(("parallel", "parallel", "parallel")),
        name=f"attn_core_d{dilation}",
    )(q, k, k, v, v, bias)


def _attn_out_kernel(x_ref, o1_ref, o2_ref, o3_ref, l1_ref, l2_ref, l3_ref, ex_ref, wo_ref,
                     fg_ref, win_ref, wout_ref, out_ref, o_scr, l_scr):
    tm = x_ref.shape[1]

    def token_order(ref, scr):
        d = ref.shape[1]
        if d == 1:
            return ref[0, 0].astype(F32)
        n_col = ref.shape[3] // LANES
        for r in range(d):
            blk = ref[0, r].astype(F32)
            for c in range(n_col):
                scr[c, pl.ds(r, tm // d, stride=d), :] = blk[:, c * LANES:(c + 1) * LANES]
        return jnp.concatenate([scr[c] for c in range(n_col)], axis=1)

    ls = [token_order(l_ref, l_scr.at[pl.ds(i, 1)]) for i, l_ref in enumerate((l1_ref, l2_ref, l3_ref))]
    m = jnp.maximum(jnp.maximum(ls[0], ls[1]), ls[2])
    es = [jnp.exp(l - m) for l in ls]
    inv = 1.0 / (es[0] + es[1] + es[2])

    def head_lanes(wgt):
        hi = wgt.astype(BF16)
        lo = (wgt - hi.astype(F32)).astype(BF16)
        return (jnp.dot(hi, ex_ref[...], preferred_element_type=F32)
                + jnp.dot(lo, ex_ref[...], preferred_element_type=F32))

    w1 = head_lanes(es[0] * inv)
    w2 = head_lanes(es[1] * inv)
    acc = w1 * token_order(o1_ref, o_scr)
    acc = acc + w2 * token_order(o2_ref, o_scr)
    acc = acc + (1.0 - w1 - w2) * token_order(o3_ref, o_scr)
    x2 = x_ref[0] + jnp.dot(acc.astype(BF16), wo_ref[...], preferred_element_type=F32)
    out_ref[0] = _ffn_tile(x2, fg_ref, win_ref, wout_ref)


def _attn_out(x, outs, lses, w_o, ffn_gain, w_in, w_out):
    B, S, D = x.shape
    dm = w_o.shape[0]
    n_heads = dm // HEAD
    tm = TOKEN_TILE
    expand = np.zeros((LSE_W, dm), np.float32)
    for hd in range(n_heads):
        expand[hd, hd * HEAD:(hd + 1) * HEAD] = 1.0
    tile = pl.BlockSpec((1, tm, D), lambda b, i: (b, i, 0))
    vec = pl.BlockSpec((1, D), lambda b, i: (0, 0))
    grouped = lambda t: pl.BlockSpec((1, t.shape[1], tm // t.shape[1], t.shape[3]), lambda b, i: (b, 0, i, 0))
    return pl.pallas_call(
        _attn_out_kernel,
        grid=(B, S // tm),
        in_specs=[tile] + [grouped(t) for t in outs] + [grouped(t) for t in lses]
                 + [_vmem_spec(), _vmem_spec(), vec, _vmem_spec(), _vmem_spec()],
        out_specs=tile,
        out_shape=jax.ShapeDtypeStruct((B, S, D), F32),
        scratch_shapes=[pltpu.VMEM((dm // LANES, tm, LANES), F32), pltpu.VMEM((len(lses), tm, LSE_W), F32)],
        compiler_params=_cparams(("parallel", "parallel")),
        name="attn_out_ffn",
    )(x, *outs, *lses, jnp.asarray(expand, dtype=BF16), w_o.astype(BF16),
      ffn_gain.reshape(1, D), w_in.astype(BF16), w_out.astype(BF16))


def _t5_bucket(dist):
    max_exact = NUM_BUCKETS // 2
    large = max_exact + (jnp.log(jnp.maximum(dist, 1).astype(F32) / max_exact)
                         / math.log(MAX_DISTANCE / max_exact) * (NUM_BUCKETS - max_exact)).astype(jnp.int32)
    large = jnp.minimum(large, NUM_BUCKETS - 1)
    return jnp.where(dist < max_exact, dist, large)


def _band_bias(table_g, window, dilation):
    steps = jnp.arange(3 * BAND - 1) - (BAND - 1)
    allowed = (steps >= 0) & (steps <= window // dilation)
    bucket = _t5_bucket(jnp.maximum(steps, 0) * dilation)
    by_step = jnp.where(allowed[:, None], jnp.take(table_g, bucket, axis=0).astype(F32) * LOG2E, NEG_INF)
    rev = by_step[::-1].T
    bias = jnp.stack([rev[:, BAND - 1 - i:3 * BAND - 1 - i] for i in range(BAND)], axis=1)
    first = jnp.where(jnp.arange(2 * BAND) >= BAND, bias, NEG_INF)
    return jnp.stack([first, bias])


def _attn_mixer(x, gain, w_qkv, q_gain, k_gain, w_o, rel_bias, ffn_gain, w_in, w_out):
    dm = w_o.shape[0]
    n_heads = dm // HEAD
    qkv = _attn_qkv(x, gain, w_qkv, q_gain, k_gain)
    outs, lses = [], []
    for g, (window, dilation) in enumerate(DILATION_PAIRS):
        q, k, v = qkv[3 * g:3 * g + 3]
        bias = _band_bias(rel_bias[:, g * n_heads:(g + 1) * n_heads], window, dilation)
        o, lse = _attn_core(q, k, v, bias, dilation)
        outs.append(o)
        lses.append(lse)
    return _attn_out(x, outs, lses, w_o, ffn_gain, w_in, w_out)


def kernel(x, mix_norm, ffn_norm, ffn_w_in, ffn_w_out, rwkv_mu, rwkv_w_rkv, rwkv_w0, rwkv_w1, rwkv_w2, rwkv_a0, rwkv_a1, rwkv_a2, rwkv_v0, rwkv_v1, rwkv_v2, rwkv_g1, rwkv_g2, rwkv_k_k, rwkv_k_a, rwkv_r_k, rwkv_ln_w, rwkv_ln_b, rwkv_w_o, attn_w_qkv, attn_q_gain, attn_k_gain, attn_w_o, rel_bias):
    depth = mix_norm.shape[0]
    v_first = None
    for layer in range(depth):
        i = layer // 2
        ffn = (ffn_norm[layer], ffn_w_in[layer], ffn_w_out[layer])
        if layer % 2 == 0:
            v_gate = None if i == 0 else (rwkv_v0[i - 1], rwkv_v1[i - 1], rwkv_v2[i - 1])
            r, k, v, lw, kn, a_sig, g = _rwkv_proj(
                x, mix_norm[layer], rwkv_mu[i], rwkv_w_rkv[i], rwkv_w0[i], rwkv_w1[i], rwkv_w2[i],
                rwkv_a0[i], rwkv_a1[i], rwkv_a2[i], rwkv_g1[i], rwkv_g2[i], rwkv_k_k[i], rwkv_k_a[i],
                v_gate, v_first)
            if v_first is None:
                v_first = v
            y = _rwkv_scan(r, k, v, lw, kn, a_sig)
            x = _rwkv_out(x, y, r, k, v, g, rwkv_ln_w[i], rwkv_ln_b[i], rwkv_r_k[i], rwkv_w_o[i], *ffn)
        else:
            x = _attn_mixer(x, mix_norm[layer], attn_w_qkv[i], attn_q_gain[i], attn_k_gain[i],
                            attn_w_o[i], rel_bias, *ffn)
    return x
```

```python
import functools
import math

import numpy as np
import jax
import jax.numpy as jnp
from jax import lax
from jax.experimental import pallas as pl
from jax.experimental.pallas import tpu as pltpu

F32 = jnp.float32
BF16 = jnp.bfloat16

HEAD = 64
LANES = 128
PAIR = 2 * HEAD
RMS_EPS = 1e-6
RWKV_LN_EPS = 64e-5
DILATION_PAIRS = ((128, 1), (512, 4), (2048, 16))
BAND = 128
NUM_BUCKETS = 32
MAX_DISTANCE = 2048
NEG_INF = -1e30
CHUNK = 64
SCAN_ROWS = 4
HEADSUM_W = 256
LSE_W = 128
CORE_GROUP = 1
CORE_BLOCKS = 4
LOG2E = math.log2(math.e)
LN2 = math.log(2.0)
TOKEN_TILE = 512
FFN_TILE = 256
VMEM_LIMIT = 56 * 1024 * 1024


def _cparams(sem):
    return pltpu.CompilerParams(dimension_semantics=sem, vmem_limit_bytes=VMEM_LIMIT)


def _vmem_spec():
    return pl.BlockSpec(memory_space=pltpu.VMEM)


def _rms(x, gain):
    return x * lax.rsqrt(jnp.mean(x * x, axis=-1, keepdims=True) + RMS_EPS) * gain


def _sigmoid(x):
    return 1.0 / (1.0 + jnp.exp(-x))


def _dot(a, b):
    return jnp.dot(a.astype(BF16), b.astype(BF16), preferred_element_type=F32)


def _dot_nt(a, b):
    return lax.dot_general(a.astype(BF16), b.astype(BF16), (((1,), (1,)), ((), ())),
                           preferred_element_type=F32)


def _dot_tn(a, b):
    return lax.dot_general(a.astype(BF16), b.astype(BF16), (((0,), (0,)), ((), ())),
                           preferred_element_type=F32)


def _headsum(x, bd_ref):
    d = x.shape[-1]
    bd = bd_ref[...]
    parts = [_dot(x[:, c:c + HEADSUM_W], bd) for c in range(0, d, HEADSUM_W)]
    return jnp.concatenate(parts, axis=-1)


def _block_diag_ones():
    i = np.arange(HEADSUM_W) // HEAD
    return jnp.asarray((i[:, None] == i[None, :]).astype(np.float32), dtype=BF16)


def _ffn_tile(x, g_ref, win_ref, wout_ref):
    d_ff = wout_ref.shape[0]
    h = _rms(x, g_ref[...]).astype(BF16)
    acc = x
    for c in range(0, d_ff, FFN_TILE):
        gate = jnp.dot(h, win_ref[:, c:c + FFN_TILE], preferred_element_type=F32)
        up = jnp.dot(h, win_ref[:, d_ff + c:d_ff + c + FFN_TILE], preferred_element_type=F32)
        act = (gate * _sigmoid(gate) * up).astype(BF16)
        acc = acc + jnp.dot(act, wout_ref[c:c + FFN_TILE, :], preferred_element_type=F32)
    return acc


def _rwkv_proj_kernel(*refs, has_vgate):
    if has_vgate:
        (x_ref, xp_ref, gain_ref, mu_ref, wrkv_ref, w0_ref, w1_ref, w2_ref, a0_ref, a1_ref, a2_ref,
         g1_ref, g2_ref, kk_ref, ka_ref, bd_ref, v0_ref, v1_ref, v2_ref, vf_ref,
         r_o, k_o, v_o, lw_o, kn_o, as_o, g_o) = refs
    else:
        (x_ref, xp_ref, gain_ref, mu_ref, wrkv_ref, w0_ref, w1_ref, w2_ref, a0_ref, a1_ref, a2_ref,
         g1_ref, g2_ref, kk_ref, ka_ref, bd_ref,
         r_o, k_o, v_o, lw_o, kn_o, as_o, g_o) = refs
    gain = gain_ref[...]
    h = _rms(x_ref[0], gain)
    hp_row = _rms(xp_ref[0], gain)[7:8, :]
    hp_row = jnp.where(pl.program_id(1) == 0, 0.0, hp_row)
    row = lax.broadcasted_iota(jnp.int32, h.shape, 0)
    h_prev = jnp.where(row == 0, hp_row, pltpu.roll(h, 1, axis=0))
    xx = h_prev - h

    def mix(c):
        return (h + xx * mu_ref[c:c + 1, :]).astype(BF16)

    xw, xa, xg, xv = mix(3), mix(4), mix(5), mix(2)
    t_w = jnp.dot(xw, w1_ref[...], preferred_element_type=F32)
    t_a = jnp.dot(xa, a1_ref[...], preferred_element_type=F32)
    t_g = jnp.dot(xg, g1_ref[...], preferred_element_type=F32)
    if has_vgate:
        t_v = jnp.dot(xv, v1_ref[...], preferred_element_type=F32)
    wl = w0_ref[...] + _dot(jnp.tanh(t_w), w2_ref[...])
    al = a0_ref[...] + _dot(t_a, a2_ref[...])
    g = _dot(_sigmoid(t_g), g2_ref[...])
    if has_vgate:
        vl = v0_ref[...] + _dot(t_v, v2_ref[...])
    xr, xk = mix(0), mix(1)
    r = jnp.dot(xr, wrkv_ref[0], preferred_element_type=F32)
    k = jnp.dot(xk, wrkv_ref[1], preferred_element_type=F32)
    v = jnp.dot(xv, wrkv_ref[2], preferred_element_type=F32)
    w = -(jnp.maximum(-wl, 0.0) + jnp.log(1.0 + jnp.exp(-jnp.abs(wl)))) - 0.5
    lw_o[0] = -jnp.exp(w)
    a = _sigmoid(al)
    if has_vgate:
        v = v + (vf_ref[0].astype(F32) - v) * _sigmoid(vl)

    kk = k * kk_ref[...]
    kn = kk * lax.rsqrt(jnp.maximum(_headsum(kk * kk, bd_ref), 1e-24))
    k = k * (1.0 + (a - 1.0) * ka_ref[...])
    r_o[0] = r.astype(BF16)
    k_o[0] = k.astype(BF16)
    v_o[0] = v.astype(BF16)
    kn_o[0] = kn.astype(BF16)
    as_o[0] = a.astype(BF16)
    g_o[0] = g.astype(BF16)


def _rwkv_proj(x, gain, mu, w_rkv, w0, w1, w2, a0, a1, a2, g1, g2, k_k, k_a, v_gate, v_first):
    B, S, D = x.shape
    tm = TOKEN_TILE
    has_vgate = v_gate is not None
    row = lambda t: t.reshape(1, D)
    tile = pl.BlockSpec((1, tm, D), lambda b, i: (b, i, 0))
    prev = pl.BlockSpec((1, 8, D), lambda b, i: (b, jnp.maximum(i * (tm // 8) - 1, 0), 0))
    vec = pl.BlockSpec((1, D), lambda b, i: (0, 0))
    args = [x, x, row(gain), mu, w_rkv.astype(BF16), row(w0), w1.astype(BF16), w2.astype(BF16),
            row(a0), a1.astype(BF16), a2.astype(BF16), g1.astype(BF16), g2.astype(BF16),
            row(k_k), row(k_a), _block_diag_ones()]
    specs = [tile, prev, vec, _vmem_spec(), _vmem_spec(), vec, _vmem_spec(), _vmem_spec(),
             vec, _vmem_spec(), _vmem_spec(), _vmem_spec(), _vmem_spec(), vec, vec, _vmem_spec()]
    if has_vgate:
        v0, v1, v2 = v_gate
        args += [row(v0), v1.astype(BF16), v2.astype(BF16), v_first]
        specs += [vec, _vmem_spec(), _vmem_spec(), tile]
    bshape = jax.ShapeDtypeStruct((B, S, D), BF16)
    out_shape = [bshape, bshape, bshape, jax.ShapeDtypeStruct((B, S, D), F32), bshape, bshape, bshape]
    return pl.pallas_call(
        functools.partial(_rwkv_proj_kernel, has_vgate=has_vgate),
        grid=(B, S // tm),
        in_specs=specs,
        out_specs=[tile] * 7,
        out_shape=out_shape,
        compiler_params=_cparams(("parallel", "parallel")),
        name="rwkv_proj",
    )(*args)


def _rwkv_scan_kernel(r_ref, k_ref, v_ref, lw_ref, kn_ref, as_ref, tri_ref, y_ref, state_ref, *,
                      n_pairs, n_rows):
    C = CHUNK

    @pl.when(pl.program_id(1) == 0)
    def _():
        state_ref[...] = jnp.zeros_like(state_ref)

    tri = tri_ref[...]
    r_d, a_d, k_d, b_d, k_e, b_e, w_tot, v = ([] for _ in range(8))
    for bi in range(n_rows):
        lw = lw_ref[bi]
        lw_hi = lw.astype(BF16)
        rem = lw - lw_hi.astype(F32)
        lw_mid = rem.astype(BF16)
        lw_lo = (rem - lw_mid.astype(F32)).astype(BF16)
        cum = (jnp.dot(tri, lw_hi, preferred_element_type=F32) + jnp.dot(tri, lw_mid, preferred_element_type=F32)
               + jnp.dot(tri, lw_lo, preferred_element_type=F32))
        tot = cum[C - 1:C, :]
        r = r_ref[bi].astype(F32)
        k = k_ref[bi].astype(F32)
        kn = kn_ref[bi].astype(F32)
        b = kn * as_ref[bi].astype(F32)
        w_inv = jnp.exp(-cum)
        w_end = jnp.exp(tot - cum)
        r_d.append((r * jnp.exp(cum)).astype(BF16))
        a_d.append((-kn * jnp.exp(cum - lw)).astype(BF16))
        k_d.append((k * w_inv).astype(BF16))
        b_d.append((b * w_inv).astype(BF16))
        k_e.append((k * w_end).astype(BF16))
        b_e.append((b * w_end).astype(BF16))
        w_tot.append(jnp.exp(tot))
        v.append(v_ref[bi])

    ri = lax.broadcasted_iota(jnp.int32, (C, PAIR), 0)
    lane = lax.broadcasted_iota(jnp.int32, (C, PAIR), 1)
    even = lane < HEAD
    ci = jnp.where(even, lane, lane - HEAD)
    strict = ri > ci
    incl = ri >= ci
    eye = jnp.where(ri == ci, 1.0, 0.0).astype(F32)

    def bdiag(yp):
        yp = yp.astype(BF16)
        zero = jnp.zeros_like(yp)
        return jnp.concatenate([jnp.where(even, yp, zero), jnp.where(even, zero, yp)], axis=0)

    ch = [(bi, j) for bi in range(n_rows) for j in range(n_pairs)]
    n = range(len(ch))
    sls = [slice(j * PAIR, (j + 1) * PAIR) for _, j in ch]
    pick = lambda arrs, c: arrs[ch[c][0]][:, sls[c]]
    xa_r = [jnp.concatenate([pick(a_d, c), pick(r_d, c)], axis=0) for c in n]
    g = [_dot_nt(xa_r[c], jnp.concatenate([bdiag(pick(k_d, c)), bdiag(pick(b_d, c))], axis=0)) for c in n]
    a_ab = [jnp.where(strict, g[c][:C, PAIR:], 0.0) for c in n]
    a_kk = [jnp.concatenate([jnp.where(strict, g[c][:C, :PAIR], 0.0), jnp.where(incl, g[c][C:, :PAIR], 0.0)],
                            axis=0).astype(BF16) for c in n]
    a_rb = [jnp.where(incl, g[c][C:, PAIR:], 0.0).astype(BF16) for c in n]
    p = [eye + a_ab[c] for c in n]
    q = [_dot(a_ab[c], bdiag(a_ab[c])) for c in n]
    for _ in range(int(math.log2(C)) - 2):
        qp = [_dot(jnp.concatenate([q[c], p[c]], axis=0), bdiag(q[c])) for c in n]
        q = [qp[c][:C] for c in n]
        p = [p[c] + qp[c][C:] for c in n]
    p = [p[c] + _dot(p[c], bdiag(q[c])) for c in n]
    s0 = [state_ref[bi, j] for bi, j in ch]
    xs = [_dot_nt(xa_r[c], bdiag(s0[c])) for c in n]
    av = [_dot(a_kk[c], bdiag(pick(v, c))) for c in n]
    u = [_dot(p[c], bdiag(xs[c][:C] + av[c][:C])) for c in n]
    y = [xs[c][C:] + av[c][C:] + _dot(a_rb[c], bdiag(u[c])) for c in n]
    for c in n:
        bi, j = ch[c]
        y_ref[bi, :, sls[c]] = y[c].astype(y_ref.dtype)
        vu = jnp.concatenate([pick(v, c), u[c].astype(BF16)], axis=0)
        kb = jnp.concatenate([pick(k_e, c), pick(b_e, c)], axis=0)
        upd = _dot_tn(vu, kb)
        state_ref[bi, j] = s0[c] * pick(w_tot, c) + jnp.where(even, upd[:HEAD], upd[HEAD:])


def _rwkv_scan(r, k, v, lw, kn, a_sig):
    B, S, D = r.shape
    n_pairs = D // PAIR
    n_rows = math.gcd(B, SCAN_ROWS)
    tile = pl.BlockSpec((n_rows, CHUNK, D), lambda b, c: (b, c, 0))
    tri = jnp.asarray(np.tril(np.ones((CHUNK, CHUNK), np.float32)), dtype=BF16)
    return pl.pallas_call(
        functools.partial(_rwkv_scan_kernel, n_pairs=n_pairs, n_rows=n_rows),
        grid=(B // n_rows, S // CHUNK),
        in_specs=[tile] * 6 + [pl.BlockSpec((CHUNK, CHUNK), lambda b, c: (0, 0))],
        out_specs=tile,
        out_shape=jax.ShapeDtypeStruct((B, S, D), BF16),
        scratch_shapes=[pltpu.VMEM((n_rows, n_pairs, HEAD, PAIR), F32)],
        compiler_params=_cparams(("parallel", "arbitrary")),
        name="rwkv_scan",
    )(r, k, v, lw, kn, a_sig, tri)


def _rwkv_out_kernel(x_ref, y_ref, r_ref, k_ref, v_ref, g_ref, lnw_ref, lnb_ref, rk_ref, bd_ref, wo_ref,
                     fg_ref, win_ref, wout_ref, o_ref):
    y = y_ref[0].astype(F32)
    inv_n = 1.0 / HEAD
    mean = _headsum(y, bd_ref) * inv_n
    d = y - mean
    var = _headsum(d * d, bd_ref) * inv_n
    yn = d * lax.rsqrt(var + RWKV_LN_EPS) * lnw_ref[...] + lnb_ref[...]
    rk = r_ref[0].astype(F32) * k_ref[0].astype(F32) * rk_ref[...]
    bonus = _headsum(rk, bd_ref) * v_ref[0].astype(F32)
    z = ((yn + bonus) * g_ref[0].astype(F32)).astype(BF16)
    x2 = x_ref[0] + jnp.dot(z, wo_ref[...], preferred_element_type=F32)
    o_ref[0] = _ffn_tile(x2, fg_ref, win_ref, wout_ref)


def _rwkv_out(x, y, r, k, v, g, ln_w, ln_b, r_k, w_o, ffn_gain, w_in, w_out):
    B, S, D = x.shape
    tm = TOKEN_TILE
    tile = pl.BlockSpec((1, tm, D), lambda b, i: (b, i, 0))
    vec = pl.BlockSpec((1, D), lambda b, i: (0, 0))
    return pl.pallas_call(
        _rwkv_out_kernel,
        grid=(B, S // tm),
        in_specs=[tile] * 6 + [vec, vec, vec, _vmem_spec(), _vmem_spec(), vec, _vmem_spec(), _vmem_spec()],
        out_specs=tile,
        out_shape=jax.ShapeDtypeStruct((B, S, D), F32),
        compiler_params=_cparams(("parallel", "parallel")),
        name="rwkv_out_ffn",
    )(x, y, r, k, v, g, ln_w.reshape(1, D), ln_b.reshape(1, D), r_k.reshape(1, D),
      _block_diag_ones(), w_o.astype(BF16), ffn_gain.reshape(1, D), w_in.astype(BF16), w_out.astype(BF16))


def _attn_qkv_kernel(x_ref, gain_ref, w_ref, qg_ref, kg_ref, bd_ref, *refs, dm, dils):
    outs, h_scr = refs[:-1], refs[-1]
    tm = x_ref.shape[1]
    h = _rms(x_ref[0], gain_ref[...])
    n_col = h.shape[1] // LANES
    for c in range(n_col):
        h_scr[c] = h[:, c * LANES:(c + 1) * LANES]
    inv_n = 1.0 / HEAD
    for g, d in enumerate(dils):
        n = tm // d
        if d == 1:
            hp = h
        else:
            hp = jnp.concatenate(
                [jnp.concatenate([h_scr[c, pl.ds(r, n, stride=d), :] for c in range(n_col)], axis=1)
                 for r in range(d)], axis=0)
        qkv = jnp.dot(hp.astype(BF16), w_ref[:, g * 3 * dm:(g + 1) * 3 * dm], preferred_element_type=F32)
        q, k, v = qkv[:, :dm], qkv[:, dm:2 * dm], qkv[:, 2 * dm:]
        q = q * lax.rsqrt(_headsum(q * q, bd_ref) * inv_n + RMS_EPS) * qg_ref[g:g + 1, :]
        k = k * lax.rsqrt(_headsum(k * k, bd_ref) * inv_n + RMS_EPS) * kg_ref[g:g + 1, :]
        for t, o_ref in zip((q, k, v), outs[3 * g:3 * g + 3]):
            t = t.astype(BF16)
            for r in range(d):
                o_ref[0, r] = t[r * n:(r + 1) * n]


def _attn_qkv(x, gain, w, q_gain, k_gain):
    B, S, D = x.shape
    dm = w.shape[1] // (3 * len(DILATION_PAIRS))
    n_heads = dm // HEAD
    tm = TOKEN_TILE
    dils = tuple(d for _, d in DILATION_PAIRS)
    scale = HEAD ** -0.5 * LOG2E
    out_shape, out_specs = [], []
    for d in dils:
        for _ in range(3):
            out_shape.append(jax.ShapeDtypeStruct((B, d, S // d, dm), BF16))
            out_specs.append(pl.BlockSpec((1, d, tm // d, dm), lambda b, i: (b, 0, i, 0)))
    vec = lambda rows, n: pl.BlockSpec((rows, n), lambda b, i: (0, 0))
    return pl.pallas_call(
        functools.partial(_attn_qkv_kernel, dm=dm, dils=dils),
        grid=(B, S // tm),
        in_specs=[pl.BlockSpec((1, tm, D), lambda b, i: (b, i, 0)), vec(1, D), _vmem_spec(),
                  vec(len(dils), dm), vec(len(dils), dm), _vmem_spec()],
        out_specs=out_specs,
        out_shape=out_shape,
        scratch_shapes=[pltpu.VMEM((D // LANES, tm, LANES), F32)],
        compiler_params=_cparams(("parallel", "parallel")),
        name="attn_qkv",
    )(x, gain.reshape(1, D), w.astype(BF16), jnp.tile(q_gain, (1, n_heads)) * scale,
      jnp.tile(k_gain, (1, n_heads)), _block_diag_ones())


def _attn_core_kernel(q_ref, kp_ref, kc_ref, vp_ref, vc_ref, bias_ref, o_ref, lse_ref, *, n_heads, n_blk):
    q = q_ref[0, 0]
    k_all = jnp.concatenate([kp_ref[0, 0], kc_ref[0, 0]], axis=0)
    v_all = jnp.concatenate([vp_ref[0, 0], vc_ref[0, 0]], axis=0)
    first_tbl = jnp.minimum(pl.program_id(2), 1)
    lane_q = lax.broadcasted_iota(jnp.int32, (BAND, PAIR), 1)
    even = lane_q < HEAD
    lane = lax.broadcasted_iota(jnp.int32, (BAND, LSE_W), 1)
    zero = jnp.zeros((BAND, PAIR), BF16)

    def scores(t, pairs):
        tbl = first_tbl if t == 0 else 1
        logits = {}
        for j in pairs:
            sl = slice(j * PAIR, (j + 1) * PAIR)
            keys = k_all[t * BAND:(t + 2) * BAND, sl]
            qp = q[t * BAND:(t + 1) * BAND, sl]
            for hd, qm in ((2 * j, jnp.where(even, qp, zero)), (2 * j + 1, jnp.where(even, zero, qp))):
                s = lax.dot_general(qm, keys, (((1,), (1,)), ((), ())), preferred_element_type=F32)
                logits[hd] = s + bias_ref[tbl, hd]
        return logits

    def softmax_parts(logits):
        parts = {}
        for hd, s in logits.items():
            m = jnp.max(s, axis=-1, keepdims=True)
            e = jnp.exp2(s - m)
            parts[hd] = (m, jnp.sum(e, axis=-1, keepdims=True), e.astype(BF16))
        return parts

    def weighted_values(t, pairs, parts, m_pad, l_pad):
        for j in pairs:
            sl = slice(j * PAIR, (j + 1) * PAIR)
            vals = v_all[t * BAND:(t + 2) * BAND, sl]
            (m_e, l_e, p_e), (m_o, l_o, p_o) = parts[2 * j], parts[2 * j + 1]
            o_e = jnp.dot(p_e, vals, preferred_element_type=F32) / l_e
            o_o = jnp.dot(p_o, vals, preferred_element_type=F32) / l_o
            o_ref[0, 0, t * BAND:(t + 1) * BAND, sl] = jnp.where(even, o_e, o_o).astype(o_ref.dtype)
            for hd, m, l in ((2 * j, m_e, l_e), (2 * j + 1, m_o, l_o)):
                m_pad = jnp.where(lane == hd, m, m_pad)
                l_pad = jnp.where(lane == hd, l, l_pad)
        return m_pad, l_pad

    n_pairs = n_heads // 2
    units = [(t, range(j0, min(j0 + CORE_GROUP, n_pairs)))
             for t in range(n_blk) for j0 in range(0, n_pairs, CORE_GROUP)]
    m_pad = [jnp.zeros((BAND, LSE_W), F32)] * n_blk
    l_pad = [jnp.ones((BAND, LSE_W), F32)] * n_blk
    logits_q, parts_q = [], []
    for step in range(len(units) + 2):
        if step < len(units):
            logits_q.append(scores(*units[step]))
        if 1 <= step <= len(units):
            parts_q.append(softmax_parts(logits_q[step - 1]))
        if step >= 2:
            t, pairs = units[step - 2]
            m_pad[t], l_pad[t] = weighted_values(t, pairs, parts_q[step - 2], m_pad[t], l_pad[t])
    for t in range(n_blk):
        lse_ref[0, 0, t * BAND:(t + 1) * BAND] = (m_pad[t] + jnp.log2(l_pad[t])) * LN2


def _attn_core(q, k, v, bias, dilation):
    B, d, sd, dm = q.shape
    n_heads = dm // HEAD
    n_blk = math.gcd(sd // BAND, CORE_BLOCKS)
    rows = n_blk * BAND
    cur = lambda w: pl.BlockSpec((1, 1, rows, w), lambda b, r, n: (b, r, n, 0))
    prv = pl.BlockSpec((1, 1, BAND, dm), lambda b, r, n: (b, r, jnp.maximum(n * n_blk - 1, 0), 0))
    return pl.pallas_call(
        functools.partial(_attn_core_kernel, n_heads=n_heads, n_blk=n_blk),
        grid=(B, d, sd // rows),
        in_specs=[cur(dm), prv, cur(dm), prv, cur(dm), _vmem_spec()],
        out_specs=[cur(dm), cur(LSE_W)],
        out_shape=[jax.ShapeDtypeStruct((B, d, sd, dm), BF16),
                   jax.ShapeDtypeStruct((B, d, sd, LSE_W), F32)],
        compiler_params=_cparams(("parallel", "parallel", "parallel")),
        name=f"attn_core_d{dilation}",
    )(q, k, k, v, v, bias)


def _attn_out_kernel(x_ref, o1_ref, o2_ref, o3_ref, l1_ref, l2_ref, l3_ref, ex_ref, wo_ref,
                     fg_ref, win_ref, wout_ref, out_ref, o_scr, l_scr):
    tm = x_ref.shape[1]

    def token_order(ref, scr):
        d = ref.shape[1]
        if d == 1:
            return ref[0, 0].astype(F32)
        n_col = ref.shape[3] // LANES
        for r in range(d):
            blk = ref[0, r].astype(F32)
            for c in range(n_col):
                scr[c, pl.ds(r, tm // d, stride=d), :] = blk[:, c * LANES:(c + 1) * LANES]
        return jnp.concatenate([scr[c] for c in range(n_col)], axis=1)

    ls = [token_order(l_ref, l_scr.at[pl.ds(i, 1)]) for i, l_ref in enumerate((l1_ref, l2_ref, l3_ref))]
    m = jnp.maximum(jnp.maximum(ls[0], ls[1]), ls[2])
    es = [jnp.exp(l - m) for l in ls]
    inv = 1.0 / (es[0] + es[1] + es[2])

    def head_lanes(wgt):
        hi = wgt.astype(BF16)
        lo = (wgt - hi.astype(F32)).astype(BF16)
        return (jnp.dot(hi, ex_ref[...], preferred_element_type=F32)
                + jnp.dot(lo, ex_ref[...], preferred_element_type=F32))

    w1 = head_lanes(es[0] * inv)
    w2 = head_lanes(es[1] * inv)
    acc = w1 * token_order(o1_ref, o_scr)
    acc = acc + w2 * token_order(o2_ref, o_scr)
    acc = acc + (1.0 - w1 - w2) * token_order(o3_ref, o_scr)
    x2 = x_ref[0] + jnp.dot(acc.astype(BF16), wo_ref[...], preferred_element_type=F32)
    out_ref[0] = _ffn_tile(x2, fg_ref, win_ref, wout_ref)


def _attn_out(x, outs, lses, w_o, ffn_gain, w_in, w_out):
    B, S, D = x.shape
    dm = w_o.shape[0]
    n_heads = dm // HEAD
    tm = TOKEN_TILE
    expand = np.zeros((LSE_W, dm), np.float32)
    for hd in range(n_heads):
        expand[hd, hd * HEAD:(hd + 1) * HEAD] = 1.0
    tile = pl.BlockSpec((1, tm, D), lambda b, i: (b, i, 0))
    vec = pl.BlockSpec((1, D), lambda b, i: (0, 0))
    grouped = lambda t: pl.BlockSpec((1, t.shape[1], tm // t.shape[1], t.shape[3]), lambda b, i: (b, 0, i, 0))
    return pl.pallas_call(
        _attn_out_kernel,
        grid=(B, S // tm),
        in_specs=[tile] + [grouped(t) for t in outs] + [grouped(t) for t in lses]
                 + [_vmem_spec(), _vmem_spec(), vec, _vmem_spec(), _vmem_spec()],
        out_specs=tile,
        out_shape=jax.ShapeDtypeStruct((B, S, D), F32),
        scratch_shapes=[pltpu.VMEM((dm // LANES, tm, LANES), F32), pltpu.VMEM((len(lses), tm, LSE_W), F32)],
        compiler_params=_cparams(("parallel", "parallel")),
        name="attn_out_ffn",
    )(x, *outs, *lses, jnp.asarray(expand, dtype=BF16), w_o.astype(BF16),
      ffn_gain.reshape(1, D), w_in.astype(BF16), w_out.astype(BF16))


def _t5_bucket(dist):
    max_exact = NUM_BUCKETS // 2
    large = max_exact + (jnp.log(jnp.maximum(dist, 1).astype(F32) / max_exact)
                         / math.log(MAX_DISTANCE / max_exact) * (NUM_BUCKETS - max_exact)).astype(jnp.int32)
    large = jnp.minimum(large, NUM_BUCKETS - 1)
    return jnp.where(dist < max_exact, dist, large)


def _band_bias(table_g, window, dilation):
    steps = jnp.arange(3 * BAND - 1) - (BAND - 1)
    allowed = (steps >= 0) & (steps <= window // dilation)
    bucket = _t5_bucket(jnp.maximum(steps, 0) * dilation)
    by_step = jnp.where(allowed[:, None], jnp.take(table_g, bucket, axis=0).astype(F32) * LOG2E, NEG_INF)
    rev = by_step[::-1].T
    bias = jnp.stack([rev[:, BAND - 1 - i:3 * BAND - 1 - i] for i in range(BAND)], axis=1)
    first = jnp.where(jnp.arange(2 * BAND) >= BAND, bias, NEG_INF)
    return jnp.stack([first, bias])


def _attn_mixer(x, gain, w_qkv, q_gain, k_gain, w_o, rel_bias, ffn_gain, w_in, w_out):
    dm = w_o.shape[0]
    n_heads = dm // HEAD
    qkv = _attn_qkv(x, gain, w_qkv, q_gain, k_gain)
    outs, lses = [], []
    for g, (window, dilation) in enumerate(DILATION_PAIRS):
        q, k, v = qkv[3 * g:3 * g + 3]
        bias = _band_bias(rel_bias[:, g * n_heads:(g + 1) * n_heads], window, dilation)
        o, lse = _attn_core(q, k, v, bias, dilation)
        outs.append(o)
        lses.append(lse)
    return _attn_out(x, outs, lses, w_o, ffn_gain, w_in, w_out)


def kernel(x, mix_norm, ffn_norm, ffn_w_in, ffn_w_out, rwkv_mu, rwkv_w_rkv, rwkv_w0, rwkv_w1, rwkv_w2, rwkv_a0, rwkv_a1, rwkv_a2, rwkv_v0, rwkv_v1, rwkv_v2, rwkv_g1, rwkv_g2, rwkv_k_k, rwkv_k_a, rwkv_r_k, rwkv_ln_w, rwkv_ln_b, rwkv_w_o, attn_w_qkv, attn_q_gain, attn_k_gain, attn_w_o, rel_bias):
    depth = mix_norm.shape[0]
    v_first = None
    for layer in range(depth):
        i = layer // 2
        ffn = (ffn_norm[layer], ffn_w_in[layer], ffn_w_out[layer])
        if layer % 2 == 0:
            v_gate = None if i == 0 else (rwkv_v0[i - 1], rwkv_v1[i - 1], rwkv_v2[i - 1])
            r, k, v, lw, kn, a_sig, g = _rwkv_proj(
                x, mix_norm[layer], rwkv_mu[i], rwkv_w_rkv[i], rwkv_w0[i], rwkv_w1[i], rwkv_w2[i],
                rwkv_a0[i], rwkv_a1[i], rwkv_a2[i], rwkv_g1[i], rwkv_g2[i], rwkv_k_k[i], rwkv_k_a[i],
                v_gate, v_first)
            if v_first is None:
                v_first = v
            y = _rwkv_scan(r, k, v, lw, kn, a_sig)
            x = _rwkv_out(x, y, r, k, v, g, rwkv_ln_w[i], rwkv_ln_b[i], rwkv_r_k[i], rwkv_w_o[i], *ffn)
        else:
            x = _attn_mixer(x, mix_norm[layer], attn_w_qkv[i], attn_q_gain[i], attn_k_gain[i],
                            attn_w_o[i], rel_bias, *ffn)
    return x
```

```python
import functools
import math

import numpy as np
import jax
import jax.numpy as jnp
from jax import lax
from jax.experimental import pallas as pl
from jax.experimental.pallas import tpu as pltpu

F32 = jnp.float32
BF16 = jnp.bfloat16

HEAD = 64
LANES = 128
SUBLANES = 8
PAIR = 2 * HEAD
RMS_EPS = 1e-6
RWKV_LN_EPS = 64e-5
DILATION_PAIRS = ((128, 1), (512, 4), (2048, 16))
BAND = 128
NUM_BUCKETS = 32
MAX_DISTANCE = 2048
NEG_INF = -1e30
CHUNK = 64
SCAN_ROWS = 4
HEADSUM_W = 256
LSE_W = 128
CORE_GROUP = 1
CORE_BLOCKS = 4
LOG2E = math.log2(math.e)
LN2 = math.log(2.0)
TOKEN_TILE = 512
FFN_TILE = 256
VMEM_LIMIT = 56 * 1024 * 1024


def _cparams(sem):
    return pltpu.CompilerParams(dimension_semantics=sem, vmem_limit_bytes=VMEM_LIMIT)


def _vmem_spec():
    return pl.BlockSpec(memory_space=pltpu.VMEM)


def _rms(x, gain):
    return x * lax.rsqrt(jnp.mean(x * x, axis=-1, keepdims=True) + RMS_EPS) * gain


def _sigmoid(x):
    return 1.0 / (1.0 + jnp.exp(-x))


def _dot(a, b):
    return jnp.dot(a.astype(BF16), b.astype(BF16), preferred_element_type=F32)


def _dot_nt(a, b):
    return lax.dot_general(a.astype(BF16), b.astype(BF16), (((1,), (1,)), ((), ())),
                           preferred_element_type=F32)


def _dot_tn(a, b):
    return lax.dot_general(a.astype(BF16), b.astype(BF16), (((0,), (0,)), ((), ())),
                           preferred_element_type=F32)


def _headsum(x, bd_ref):
    d = x.shape[-1]
    bd = bd_ref[...]
    parts = [_dot(x[:, c:c + HEADSUM_W], bd) for c in range(0, d, HEADSUM_W)]
    return jnp.concatenate(parts, axis=-1)


def _block_diag_ones():
    i = np.arange(HEADSUM_W) // HEAD
    return jnp.asarray((i[:, None] == i[None, :]).astype(np.float32), dtype=BF16)


def _ffn_tile(x, g_ref, win_ref, wout_ref):
    d_ff = wout_ref.shape[0]
    h = _rms(x, g_ref[...]).astype(BF16)
    acc = x
    for c in range(0, d_ff, FFN_TILE):
        gate = jnp.dot(h, win_ref[:, c:c + FFN_TILE], preferred_element_type=F32)
        up = jnp.dot(h, win_ref[:, d_ff + c:d_ff + c + FFN_TILE], preferred_element_type=F32)
        act = (gate * _sigmoid(gate) * up).astype(BF16)
        acc = acc + jnp.dot(act, wout_ref[c:c + FFN_TILE, :], preferred_element_type=F32)
    return acc


def _rwkv_proj_kernel(*refs, has_vgate):
    if has_vgate:
        (x_ref, xp_ref, gain_ref, mu_ref, wrkv_ref, w0_ref, w1_ref, w2_ref, a0_ref, a1_ref, a2_ref,
         g1_ref, g2_ref, kk_ref, ka_ref, bd_ref, v0_ref, v1_ref, v2_ref, vf_ref,
         r_o, k_o, v_o, lw_o, kn_o, as_o, g_o) = refs
    else:
        (x_ref, xp_ref, gain_ref, mu_ref, wrkv_ref, w0_ref, w1_ref, w2_ref, a0_ref, a1_ref, a2_ref,
         g1_ref, g2_ref, kk_ref, ka_ref, bd_ref,
         r_o, k_o, v_o, lw_o, kn_o, as_o, g_o) = refs
    gain = gain_ref[...]
    h = _rms(x_ref[0], gain)
    hp_row = _rms(xp_ref[0], gain)[SUBLANES - 1:, :]
    hp_row = jnp.where(pl.program_id(1) == 0, 0.0, hp_row)
    row = lax.broadcasted_iota(jnp.int32, h.shape, 0)
    h_prev = jnp.where(row == 0, hp_row, pltpu.roll(h, 1, axis=0))
    xx = h_prev - h

    def mix(c):
        return (h + xx * mu_ref[c:c + 1, :]).astype(BF16)

    xw, xa, xg, xv = mix(3), mix(4), mix(5), mix(2)
    t_w = jnp.dot(xw, w1_ref[...], preferred_element_type=F32)
    t_a = jnp.dot(xa, a1_ref[...], preferred_element_type=F32)
    t_g = jnp.dot(xg, g1_ref[...], preferred_element_type=F32)
    if has_vgate:
        t_v = jnp.dot(xv, v1_ref[...], preferred_element_type=F32)
    wl = w0_ref[...] + _dot(jnp.tanh(t_w), w2_ref[...])
    al = a0_ref[...] + _dot(t_a, a2_ref[...])
    g = _dot(_sigmoid(t_g), g2_ref[...])
    if has_vgate:
        vl = v0_ref[...] + _dot(t_v, v2_ref[...])
    xr, xk = mix(0), mix(1)
    r = jnp.dot(xr, wrkv_ref[0], preferred_element_type=F32)
    k = jnp.dot(xk, wrkv_ref[1], preferred_element_type=F32)
    v = jnp.dot(xv, wrkv_ref[2], preferred_element_type=F32)
    lw_o[0] = -math.exp(-0.5) * _sigmoid(wl)
    a = _sigmoid(al)
    if has_vgate:
        v = v + (vf_ref[0].astype(F32) - v) * _sigmoid(vl)

    kk = k * kk_ref[...]
    kn = kk * lax.rsqrt(jnp.maximum(_headsum(kk * kk, bd_ref), 1e-24))
    k = k * (1.0 + (a - 1.0) * ka_ref[...])
    r_o[0] = r.astype(BF16)
    k_o[0] = k.astype(BF16)
    v_o[0] = v.astype(BF16)
    kn_o[0] = kn.astype(BF16)
    as_o[0] = a.astype(BF16)
    g_o[0] = g.astype(BF16)


def _rwkv_proj(x, gain, mu, w_rkv, w0, w1, w2, a0, a1, a2, g1, g2, k_k, k_a, v_gate, v_first):
    B, S, D = x.shape
    tm = TOKEN_TILE
    has_vgate = v_gate is not None
    row = lambda t: t.reshape(1, D)
    tile = pl.BlockSpec((1, tm, D), lambda b, i: (b, i, 0))
    prev = pl.BlockSpec((1, SUBLANES, D), lambda b, i: (b, jnp.maximum(i * (tm // SUBLANES) - 1, 0), 0))
    vec = pl.BlockSpec((1, D), lambda b, i: (0, 0))
    args = [x, x, row(gain), mu, w_rkv.astype(BF16), row(w0), w1.astype(BF16), w2.astype(BF16),
            row(a0), a1.astype(BF16), a2.astype(BF16), g1.astype(BF16), g2.astype(BF16),
            row(k_k), row(k_a), _block_diag_ones()]
    specs = [tile, prev, vec, _vmem_spec(), _vmem_spec(), vec, _vmem_spec(), _vmem_spec(),
             vec, _vmem_spec(), _vmem_spec(), _vmem_spec(), _vmem_spec(), vec, vec, _vmem_spec()]
    if has_vgate:
        v0, v1, v2 = v_gate
        args += [row(v0), v1.astype(BF16), v2.astype(BF16), v_first]
        specs += [vec, _vmem_spec(), _vmem_spec(), tile]
    bshape = jax.ShapeDtypeStruct((B, S, D), BF16)
    out_shape = [bshape, bshape, bshape, jax.ShapeDtypeStruct((B, S, D), F32), bshape, bshape, bshape]
    return pl.pallas_call(
        functools.partial(_rwkv_proj_kernel, has_vgate=has_vgate),
        grid=(B, S // tm),
        in_specs=specs,
        out_specs=[tile] * 7,
        out_shape=out_shape,
        compiler_params=_cparams(("parallel", "parallel")),
        name="rwkv_proj",
    )(*args)


def _rwkv_scan_kernel(r_ref, k_ref, v_ref, lw_ref, kn_ref, as_ref, tri_ref, y_ref, state_ref, *,
                      n_pairs, n_rows):
    C = CHUNK

    @pl.when(pl.program_id(1) == 0)
    def _():
        state_ref[...] = jnp.zeros_like(state_ref)

    tri = tri_ref[...]
    r_d, a_d, k_d, b_d, k_e, b_e, w_tot, v = ([] for _ in range(8))
    for bi in range(n_rows):
        lw = lw_ref[bi]
        lw_hi = lw.astype(BF16)
        rem = lw - lw_hi.astype(F32)
        lw_mid = rem.astype(BF16)
        lw_lo = (rem - lw_mid.astype(F32)).astype(BF16)
        cum = (jnp.dot(tri, lw_hi, preferred_element_type=F32) + jnp.dot(tri, lw_mid, preferred_element_type=F32)
               + jnp.dot(tri, lw_lo, preferred_element_type=F32))
        tot = cum[C - 1:C, :]
        r = r_ref[bi].astype(F32)
        k = k_ref[bi].astype(F32)
        kn = kn_ref[bi].astype(F32)
        b = kn * as_ref[bi].astype(F32)
        w_inv = jnp.exp(-cum)
        w_end = jnp.exp(tot - cum)
        r_d.append((r * jnp.exp(cum)).astype(BF16))
        a_d.append((-kn * jnp.exp(cum - lw)).astype(BF16))
        k_d.append((k * w_inv).astype(BF16))
        b_d.append((b * w_inv).astype(BF16))
        k_e.append((k * w_end).astype(BF16))
        b_e.append((b * w_end).astype(BF16))
        w_tot.append(jnp.exp(tot))
        v.append(v_ref[bi])

    ri = lax.broadcasted_iota(jnp.int32, (C, PAIR), 0)
    lane = lax.broadcasted_iota(jnp.int32, (C, PAIR), 1)
    even = lane < HEAD
    ci = jnp.where(even, lane, lane - HEAD)
    strict = ri > ci
    incl = ri >= ci
    eye = jnp.where(ri == ci, 1.0, 0.0).astype(F32)

    def bdiag(yp):
        yp = yp.astype(BF16)
        zero = jnp.zeros_like(yp)
        return jnp.concatenate([jnp.where(even, yp, zero), jnp.where(even, zero, yp)], axis=0)

    ch = [(bi, j) for bi in range(n_rows) for j in range(n_pairs)]
    n = range(len(ch))
    sls = [slice(j * PAIR, (j + 1) * PAIR) for _, j in ch]
    pick = lambda arrs, c: arrs[ch[c][0]][:, sls[c]]
    xa_r = [jnp.concatenate([pick(a_d, c), pick(r_d, c)], axis=0) for c in n]
    g = [_dot_nt(xa_r[c], jnp.concatenate([bdiag(pick(k_d, c)), bdiag(pick(b_d, c))], axis=0)) for c in n]
    a_ab = [jnp.where(strict, g[c][:C, PAIR:], 0.0) for c in n]
    a_kk = [jnp.concatenate([jnp.where(strict, g[c][:C, :PAIR], 0.0), jnp.where(incl, g[c][C:, :PAIR], 0.0)],
                            axis=0).astype(BF16) for c in n]
    a_rb = [jnp.where(incl, g[c][C:, PAIR:], 0.0).astype(BF16) for c in n]
    p = [eye + a_ab[c] for c in n]
    q = [_dot(a_ab[c], bdiag(a_ab[c])) for c in n]
    for _ in range(int(math.log2(C)) - 2):
        qp = [_dot(jnp.concatenate([q[c], p[c]], axis=0), bdiag(q[c])) for c in n]
        q = [qp[c][:C] for c in n]
        p = [p[c] + qp[c][C:] for c in n]
    p = [p[c] + _dot(p[c], bdiag(q[c])) for c in n]
    s0 = [state_ref[bi, j] for bi, j in ch]
    xs = [_dot_nt(xa_r[c], bdiag(s0[c])) for c in n]
    av = [_dot(a_kk[c], bdiag(pick(v, c))) for c in n]
    u = [_dot(p[c], bdiag(xs[c][:C] + av[c][:C])) for c in n]
    y = [xs[c][C:] + av[c][C:] + _dot(a_rb[c], bdiag(u[c])) for c in n]
    for c in n:
        bi, j = ch[c]
        y_ref[bi, :, sls[c]] = y[c].astype(y_ref.dtype)
        vu = jnp.concatenate([pick(v, c), u[c].astype(BF16)], axis=0)
        kb = jnp.concatenate([pick(k_e, c), pick(b_e, c)], axis=0)
        upd = _dot_tn(vu, kb)
        state_ref[bi, j] = s0[c] * pick(w_tot, c) + jnp.where(even, upd[:HEAD], upd[HEAD:])


def _rwkv_scan(r, k, v, lw, kn, a_sig):
    B, S, D = r.shape
    n_pairs = D // PAIR
    n_rows = math.gcd(B, SCAN_ROWS)
    tile = pl.BlockSpec((n_rows, CHUNK, D), lambda b, c: (b, c, 0))
    tri = jnp.asarray(np.tril(np.ones((CHUNK, CHUNK), np.float32)), dtype=BF16)
    return pl.pallas_call(
        functools.partial(_rwkv_scan_kernel, n_pairs=n_pairs, n_rows=n_rows),
        grid=(B // n_rows, S // CHUNK),
        in_specs=[tile] * 6 + [pl.BlockSpec((CHUNK, CHUNK), lambda b, c: (0, 0))],
        out_specs=tile,
        out_shape=jax.ShapeDtypeStruct((B, S, D), BF16),
        scratch_shapes=[pltpu.VMEM((n_rows, n_pairs, HEAD, PAIR), F32)],
        compiler_params=_cparams(("parallel", "arbitrary")),
        name="rwkv_scan",
    )(r, k, v, lw, kn, a_sig, tri)


def _rwkv_out_kernel(x_ref, y_ref, r_ref, k_ref, v_ref, g_ref, lnw_ref, lnb_ref, rk_ref, bd_ref, wo_ref,
                     fg_ref, win_ref, wout_ref, o_ref):
    y = y_ref[0].astype(F32)
    inv_n = 1.0 / HEAD
    mean = _headsum(y, bd_ref) * inv_n
    d = y - mean
    var = _headsum(d * d, bd_ref) * inv_n
    yn = d * lax.rsqrt(var + RWKV_LN_EPS) * lnw_ref[...] + lnb_ref[...]
    rk = r_ref[0].astype(F32) * k_ref[0].astype(F32) * rk_ref[...]
    bonus = _headsum(rk, bd_ref) * v_ref[0].astype(F32)
    z = ((yn + bonus) * g_ref[0].astype(F32)).astype(BF16)
    x2 = x_ref[0] + jnp.dot(z, wo_ref[...], preferred_element_type=F32)
    o_ref[0] = _ffn_tile(x2, fg_ref, win_ref, wout_ref)


def _rwkv_out(x, y, r, k, v, g, ln_w, ln_b, r_k, w_o, ffn_gain, w_in, w_out):
    B, S, D = x.shape
    tm = TOKEN_TILE
    tile = pl.BlockSpec((1, tm, D), lambda b, i: (b, i, 0))
    vec = pl.BlockSpec((1, D), lambda b, i: (0, 0))
    return pl.pallas_call(
        _rwkv_out_kernel,
        grid=(B, S // tm),
        in_specs=[tile] * 6 + [vec, vec, vec, _vmem_spec(), _vmem_spec(), vec, _vmem_spec(), _vmem_spec()],
        out_specs=tile,
        out_shape=jax.ShapeDtypeStruct((B, S, D), F32),
        compiler_params=_cparams(("parallel", "parallel")),
        name="rwkv_out_ffn",
    )(x, y, r, k, v, g, ln_w.reshape(1, D), ln_b.reshape(1, D), r_k.reshape(1, D),
      _block_diag_ones(), w_o.astype(BF16), ffn_gain.reshape(1, D), w_in.astype(BF16), w_out.astype(BF16))


def _attn_qkv_kernel(x_ref, gain_ref, w_ref, qg_ref, kg_ref, bd_ref, *refs, dm, dils):
    outs, h_scr = refs[:-1], refs[-1]
    tm = x_ref.shape[1]
    h = _rms(x_ref[0], gain_ref[...])
    n_col = h.shape[1] // LANES
    for c in range(n_col):
        h_scr[c] = h[:, c * LANES:(c + 1) * LANES]
    inv_n = 1.0 / HEAD
    for g, d in enumerate(dils):
        n = tm // d
        if d == 1:
            hp = h
        else:
            hp = jnp.concatenate(
                [jnp.concatenate([h_scr[c, pl.ds(r, n, stride=d), :] for c in range(n_col)], axis=1)
                 for r in range(d)], axis=0)
        qkv = jnp.dot(hp.astype(BF16), w_ref[:, g * 3 * dm:(g + 1) * 3 * dm], preferred_element_type=F32)
        q, k, v = qkv[:, :dm], qkv[:, dm:2 * dm], qkv[:, 2 * dm:]
        q = q * lax.rsqrt(_headsum(q * q, bd_ref) * inv_n + RMS_EPS) * qg_ref[g:g + 1, :]
        k = k * lax.rsqrt(_headsum(k * k, bd_ref) * inv_n + RMS_EPS) * kg_ref[g:g + 1, :]
        for t, o_ref in zip((q, k, v), outs[3 * g:3 * g + 3]):
            t = t.astype(BF16)
            for r in range(d):
                o_ref[0, r] = t[r * n:(r + 1) * n]


def _attn_qkv(x, gain, w, q_gain, k_gain):
    B, S, D = x.shape
    dm = w.shape[1] // (3 * len(DILATION_PAIRS))
    n_heads = dm // HEAD
    tm = TOKEN_TILE
    dils = tuple(d for _, d in DILATION_PAIRS)
    scale = HEAD ** -0.5 * LOG2E
    out_shape, out_specs = [], []
    for d in dils:
        for _ in range(3):
            out_shape.append(jax.ShapeDtypeStruct((B, d, S // d, dm), BF16))
            out_specs.append(pl.BlockSpec((1, d, tm // d, dm), lambda b, i: (b, 0, i, 0)))
    vec = lambda rows, n: pl.BlockSpec((rows, n), lambda b, i: (0, 0))
    return pl.pallas_call(
        functools.partial(_attn_qkv_kernel, dm=dm, dils=dils),
        grid=(B, S // tm),
        in_specs=[pl.BlockSpec((1, tm, D), lambda b, i: (b, i, 0)), vec(1, D), _vmem_spec(),
                  vec(len(dils), dm), vec(len(dils), dm), _vmem_spec()],
        out_specs=out_specs,
        out_shape=out_shape,
        scratch_shapes=[pltpu.VMEM((D // LANES, tm, LANES), F32)],
        compiler_params=_cparams(("parallel", "parallel")),
        name="attn_qkv",
    )(x, gain.reshape(1, D), w.astype(BF16), jnp.tile(q_gain, (1, n_heads)) * scale,
      jnp.tile(k_gain, (1, n_heads)), _block_diag_ones())


def _attn_core_kernel(q_ref, kp_ref, kc_ref, vp_ref, vc_ref, bias_ref, o_ref, lse_ref, *, n_heads, n_blk):
    q = q_ref[0, 0]
    k_all = jnp.concatenate([kp_ref[0, 0], kc_ref[0, 0]], axis=0)
    v_all = jnp.concatenate([vp_ref[0, 0], vc_ref[0, 0]], axis=0)
    first_tbl = jnp.minimum(pl.program_id(2), 1)
    lane_q = lax.broadcasted_iota(jnp.int32, (BAND, PAIR), 1)
    even = lane_q < HEAD
    lane = lax.broadcasted_iota(jnp.int32, (BAND, LSE_W), 1)
    zero = jnp.zeros((BAND, PAIR), BF16)

    def scores(t, pairs):
        tbl = first_tbl if t == 0 else 1
        logits = {}
        for j in pairs:
            sl = slice(j * PAIR, (j + 1) * PAIR)
            keys = k_all[t * BAND:(t + 2) * BAND, sl]
            qp = q[t * BAND:(t + 1) * BAND, sl]
            for hd, qm in ((2 * j, jnp.where(even, qp, zero)), (2 * j + 1, jnp.where(even, zero, qp))):
                s = lax.dot_general(qm, keys, (((1,), (1,)), ((), ())), preferred_element_type=F32)
                logits[hd] = s + bias_ref[tbl, hd]
        return logits

    def softmax_parts(logits):
        parts = {}
        for hd, s in logits.items():
            m = jnp.max(s, axis=-1, keepdims=True)
            e = jnp.exp2(s - m)
            parts[hd] = (m, jnp.sum(e, axis=-1, keepdims=True), e.astype(BF16))
        return parts

    def weighted_values(t, pairs, parts, m_pad, l_pad):
        for j in pairs:
            sl = slice(j * PAIR, (j + 1) * PAIR)
            vals = v_all[t * BAND:(t + 2) * BAND, sl]
            (m_e, l_e, p_e), (m_o, l_o, p_o) = parts[2 * j], parts[2 * j + 1]
            o_e = jnp.dot(p_e, vals, preferred_element_type=F32) / l_e
            o_o = jnp.dot(p_o, vals, preferred_element_type=F32) / l_o
            o_ref[0, 0, t * BAND:(t + 1) * BAND, sl] = jnp.where(even, o_e, o_o).astype(o_ref.dtype)
            for hd, m, l in ((2 * j, m_e, l_e), (2 * j + 1, m_o, l_o)):
                m_pad = jnp.where(lane == hd, m, m_pad)
                l_pad = jnp.where(lane == hd, l, l_pad)
        return m_pad, l_pad

    n_pairs = n_heads // 2
    units = [(t, range(j0, min(j0 + CORE_GROUP, n_pairs)))
             for t in range(n_blk) for j0 in range(0, n_pairs, CORE_GROUP)]
    m_pad = [jnp.zeros((BAND, LSE_W), F32)] * n_blk
    l_pad = [jnp.ones((BAND, LSE_W), F32)] * n_blk
    logits_q, parts_q = [], []
    for step in range(len(units) + 2):
        if step < len(units):
            logits_q.append(scores(*units[step]))
        if 1 <= step <= len(units):
            parts_q.append(softmax_parts(logits_q[step - 1]))
        if step >= 2:
            t, pairs = units[step - 2]
            m_pad[t], l_pad[t] = weighted_values(t, pairs, parts_q[step - 2], m_pad[t], l_pad[t])
    for t in range(n_blk):
        lse_ref[0, 0, t * BAND:(t + 1) * BAND] = (m_pad[t] + jnp.log2(l_pad[t])) * LN2


def _attn_core(q, k, v, bias, dilation):
    B, d, sd, dm = q.shape
    n_heads = dm // HEAD
    n_blk = math.gcd(sd // BAND, CORE_BLOCKS)
    rows = n_blk * BAND
    cur = lambda w: pl.BlockSpec((1, 1, rows, w), lambda b, r, n: (b, r, n, 0))
    prv = pl.BlockSpec((1, 1, BAND, dm), lambda b, r, n: (b, r, jnp.maximum(n * n_blk - 1, 0), 0))
    return pl.pallas_call(
        functools.partial(_attn_core_kernel, n_heads=n_heads, n_blk=n_blk),
        grid=(B, d, sd // rows),
        in_specs=[cur(dm), prv, cur(dm), prv, cur(dm), _vmem_spec()],
        out_specs=[cur(dm), cur(LSE_W)],
        out_shape=[jax.ShapeDtypeStruct((B, d, sd, dm), BF16),
                   jax.ShapeDtypeStruct((B, d, sd, LSE_W), F32)],
        compiler_params=_cparams(("parallel", "parallel", "parallel")),
        name=f"attn_core_d{dilation}",
    )(q, k, k, v, v, bias)


def _attn_out_kernel(x_ref, o1_ref, o2_ref, o3_ref, l1_ref, l2_ref, l3_ref, ex_ref, wo_ref,
                     fg_ref, win_ref, wout_ref, out_ref, o_scr, l_scr):
    tm = x_ref.shape[1]

    def token_order(ref, scr):
        d = ref.shape[1]
        if d == 1:
            return ref[0, 0].astype(F32)
        n_col = ref.shape[3] // LANES
        for r in range(d):
            blk = ref[0, r].astype(F32)
            for c in range(n_col):
                scr[c, pl.ds(r, tm // d, stride=d), :] = blk[:, c * LANES:(c + 1) * LANES]
        return jnp.concatenate([scr[c] for c in range(n_col)], axis=1)

    ls = [token_order(l_ref, l_scr.at[pl.ds(i, 1)]) for i, l_ref in enumerate((l1_ref, l2_ref, l3_ref))]
    m = jnp.maximum(jnp.maximum(ls[0], ls[1]), ls[2])
    es = [jnp.exp(l - m) for l in ls]
    inv = 1.0 / (es[0] + es[1] + es[2])

    def head_lanes(wgt):
        hi = wgt.astype(BF16)
        lo = (wgt - hi.astype(F32)).astype(BF16)
        return (jnp.dot(hi, ex_ref[...], preferred_element_type=F32)
                + jnp.dot(lo, ex_ref[...], preferred_element_type=F32))

    w1 = head_lanes(es[0] * inv)
    w2 = head_lanes(es[1] * inv)
    acc = w1 * token_order(o1_ref, o_scr)
    acc = acc + w2 * token_order(o2_ref, o_scr)
    acc = acc + (1.0 - w1 - w2) * token_order(o3_ref, o_scr)
    x2 = x_ref[0] + jnp.dot(acc.astype(BF16), wo_ref[...], preferred_element_type=F32)
    out_ref[0] = _ffn_tile(x2, fg_ref, win_ref, wout_ref)


def _attn_out(x, outs, lses, w_o, ffn_gain, w_in, w_out):
    B, S, D = x.shape
    dm = w_o.shape[0]
    n_heads = dm // HEAD
    tm = TOKEN_TILE
    expand = np.zeros((LSE_W, dm), np.float32)
    for hd in range(n_heads):
        expand[hd, hd * HEAD:(hd + 1) * HEAD] = 1.0
    tile = pl.BlockSpec((1, tm, D), lambda b, i: (b, i, 0))
    vec = pl.BlockSpec((1, D), lambda b, i: (0, 0))
    grouped = lambda t: pl.BlockSpec((1, t.shape[1], tm // t.shape[1], t.shape[3]), lambda b, i: (b, 0, i, 0))
    return pl.pallas_call(
        _attn_out_kernel,
        grid=(B, S // tm),
        in_specs=[tile] + [grouped(t) for t in outs] + [grouped(t) for t in lses]
                 + [_vmem_spec(), _vmem_spec(), vec, _vmem_spec(), _vmem_spec()],
        out_specs=tile,
        out_shape=jax.ShapeDtypeStruct((B, S, D), F32),
        scratch_shapes=[pltpu.VMEM((dm // LANES, tm, LANES), F32), pltpu.VMEM((len(lses), tm, LSE_W), F32)],
        compiler_params=_cparams(("parallel", "parallel")),
        name="attn_out_ffn",
    )(x, *outs, *lses, jnp.asarray(expand, dtype=BF16), w_o.astype(BF16),
      ffn_gain.reshape(1, D), w_in.astype(BF16), w_out.astype(BF16))


def _t5_bucket(dist):
    max_exact = NUM_BUCKETS // 2
    large = max_exact + (jnp.log(jnp.maximum(dist, 1).astype(F32) / max_exact)
                         / math.log(MAX_DISTANCE / max_exact) * (NUM_BUCKETS - max_exact)).astype(jnp.int32)
    large = jnp.minimum(large, NUM_BUCKETS - 1)
    return jnp.where(dist < max_exact, dist, large)


def _band_bias(table_g, window, dilation):
    steps = jnp.arange(3 * BAND - 1) - (BAND - 1)
    allowed = (steps >= 0) & (steps <= window // dilation)
    bucket = _t5_bucket(jnp.maximum(steps, 0) * dilation)
    by_step = jnp.where(allowed[:, None], jnp.take(table_g, bucket, axis=0).astype(F32) * LOG2E, NEG_INF)
    rev = by_step[::-1].T
    bias = jnp.stack([rev[:, BAND - 1 - i:3 * BAND - 1 - i] for i in range(BAND)], axis=1)
    first = jnp.where(jnp.arange(2 * BAND) >= BAND, bias, NEG_INF)
    return jnp.stack([first, bias])


def _attn_mixer(x, gain, w_qkv, q_gain, k_gain, w_o, rel_bias, ffn_gain, w_in, w_out):
    dm = w_o.shape[0]
    n_heads = dm // HEAD
    qkv = _attn_qkv(x, gain, w_qkv, q_gain, k_gain)
    outs, lses = [], []
    for g, (window, dilation) in enumerate(DILATION_PAIRS):
        q, k, v = qkv[3 * g:3 * g + 3]
        bias = _band_bias(rel_bias[:, g * n_heads:(g + 1) * n_heads], window, dilation)
        o, lse = _attn_core(q, k, v, bias, dilation)
        outs.append(o)
        lses.append(lse)
    return _attn_out(x, outs, lses, w_o, ffn_gain, w_in, w_out)


def kernel(x, mix_norm, ffn_norm, ffn_w_in, ffn_w_out, rwkv_mu, rwkv_w_rkv, rwkv_w0, rwkv_w1, rwkv_w2, rwkv_a0, rwkv_a1, rwkv_a2, rwkv_v0, rwkv_v1, rwkv_v2, rwkv_g1, rwkv_g2, rwkv_k_k, rwkv_k_a, rwkv_r_k, rwkv_ln_w, rwkv_ln_b, rwkv_w_o, attn_w_qkv, attn_q_gain, attn_k_gain, attn_w_o, rel_bias):
    depth = mix_norm.shape[0]
    v_first = None
    for layer in range(depth):
        i = layer // 2
        ffn = (ffn_norm[layer], ffn_w_in[layer], ffn_w_out[layer])
        if layer % 2 == 0:
            v_gate = None if i == 0 else (rwkv_v0[i - 1], rwkv_v1[i - 1], rwkv_v2[i - 1])
            r, k, v, lw, kn, a_sig, g = _rwkv_proj(
                x, mix_norm[layer], rwkv_mu[i], rwkv_w_rkv[i], rwkv_w0[i], rwkv_w1[i], rwkv_w2[i],
                rwkv_a0[i], rwkv_a1[i], rwkv_a2[i], rwkv_g1[i], rwkv_g2[i], rwkv_k_k[i], rwkv_k_a[i],
                v_gate, v_first)
            if v_first is None:
                v_first = v
            y = _rwkv_scan(r, k, v, lw, kn, a_sig)
            x = _rwkv_out(x, y, r, k, v, g, rwkv_ln_w[i], rwkv_ln_b[i], rwkv_r_k[i], rwkv_w_o[i], *ffn)
        else:
            x = _attn_mixer(x, mix_norm[layer], attn_w_qkv[i], attn_q_gain[i], attn_k_gain[i],
                            attn_w_o[i], rel_bias, *ffn)
    return x
```

```python
import functools
import math

import numpy as np
import jax
import jax.numpy as jnp
from jax import lax
from jax.experimental import pallas as pl
from jax.experimental.pallas import tpu as pltpu

F32 = jnp.float32
BF16 = jnp.bfloat16

HEAD = 64
LANES = 128
SUBLANES = 8
PAIR = 2 * HEAD
RMS_EPS = 1e-6
RWKV_LN_EPS = 64e-5
DILATION_PAIRS = ((128, 1), (512, 4), (2048, 16))
BAND = 128
NUM_BUCKETS = 32
MAX_DISTANCE = 2048
NEG_INF = -1e30
CHUNK = 64
SCAN_ROWS = 8
HEADSUM_W = 256
LSE_W = 128
CORE_GROUP = 1
CORE_BLOCKS = 4
LOG2E = math.log2(math.e)
LN2 = math.log(2.0)
TOKEN_TILE = 512
FFN_TILE = 256
VMEM_LIMIT = 56 * 1024 * 1024


def _cparams(sem):
    return pltpu.CompilerParams(dimension_semantics=sem, vmem_limit_bytes=VMEM_LIMIT)


def _vmem_spec():
    return pl.BlockSpec(memory_space=pltpu.VMEM)


def _rms(x, gain):
    return x * lax.rsqrt(jnp.mean(x * x, axis=-1, keepdims=True) + RMS_EPS) * gain


def _sigmoid(x):
    return 1.0 / (1.0 + jnp.exp(-x))


def _dot(a, b):
    return jnp.dot(a.astype(BF16), b.astype(BF16), preferred_element_type=F32)


def _dot_nt(a, b):
    return lax.dot_general(a.astype(BF16), b.astype(BF16), (((1,), (1,)), ((), ())),
                           preferred_element_type=F32)


def _dot_tn(a, b):
    return lax.dot_general(a.astype(BF16), b.astype(BF16), (((0,), (0,)), ((), ())),
                           preferred_element_type=F32)


def _headsum(x, bd_ref):
    d = x.shape[-1]
    bd = bd_ref[...]
    parts = [_dot(x[:, c:c + HEADSUM_W], bd) for c in range(0, d, HEADSUM_W)]
    return jnp.concatenate(parts, axis=-1)


def _block_diag_ones():
    i = np.arange(HEADSUM_W) // HEAD
    return jnp.asarray((i[:, None] == i[None, :]).astype(np.float32), dtype=BF16)


def _ffn_tile(x, g_ref, win_ref, wout_ref):
    d_ff = wout_ref.shape[0]
    h = _rms(x, g_ref[...]).astype(BF16)
    acc = x
    for c in range(0, d_ff, FFN_TILE):
        gate = jnp.dot(h, win_ref[:, c:c + FFN_TILE], preferred_element_type=F32)
        up = jnp.dot(h, win_ref[:, d_ff + c:d_ff + c + FFN_TILE], preferred_element_type=F32)
        act = (gate * _sigmoid(gate) * up).astype(BF16)
        acc = acc + jnp.dot(act, wout_ref[c:c + FFN_TILE, :], preferred_element_type=F32)
    return acc


def _rwkv_proj_kernel(*refs, has_vgate):
    if has_vgate:
        (x_ref, xp_ref, gain_ref, mu_ref, wrkv_ref, w0_ref, w1_ref, w2_ref, a0_ref, a1_ref, a2_ref,
         g1_ref, g2_ref, kk_ref, ka_ref, bd_ref, v0_ref, v1_ref, v2_ref, vf_ref,
         r_o, k_o, v_o, lw_o, kn_o, as_o, g_o) = refs
    else:
        (x_ref, xp_ref, gain_ref, mu_ref, wrkv_ref, w0_ref, w1_ref, w2_ref, a0_ref, a1_ref, a2_ref,
         g1_ref, g2_ref, kk_ref, ka_ref, bd_ref,
         r_o, k_o, v_o, lw_o, kn_o, as_o, g_o) = refs
    gain = gain_ref[...]
    h = _rms(x_ref[0], gain)
    hp_row = _rms(xp_ref[0], gain)[SUBLANES - 1:, :]
    hp_row = jnp.where(pl.program_id(1) == 0, 0.0, hp_row)
    row = lax.broadcasted_iota(jnp.int32, h.shape, 0)
    h_prev = jnp.where(row == 0, hp_row, pltpu.roll(h, 1, axis=0))
    xx = h_prev - h

    def mix(c):
        return (h + xx * mu_ref[c:c + 1, :]).astype(BF16)

    xw, xa, xg, xv = mix(3), mix(4), mix(5), mix(2)
    t_w = jnp.dot(xw, w1_ref[...], preferred_element_type=F32)
    t_a = jnp.dot(xa, a1_ref[...], preferred_element_type=F32)
    t_g = jnp.dot(xg, g1_ref[...], preferred_element_type=F32)
    if has_vgate:
        t_v = jnp.dot(xv, v1_ref[...], preferred_element_type=F32)
    wl = w0_ref[...] + _dot(jnp.tanh(t_w), w2_ref[...])
    al = a0_ref[...] + _dot(t_a, a2_ref[...])
    g = _dot(_sigmoid(t_g), g2_ref[...])
    if has_vgate:
        vl = v0_ref[...] + _dot(t_v, v2_ref[...])
    xr, xk = mix(0), mix(1)
    r = jnp.dot(xr, wrkv_ref[0], preferred_element_type=F32)
    k = jnp.dot(xk, wrkv_ref[1], preferred_element_type=F32)
    v = jnp.dot(xv, wrkv_ref[2], preferred_element_type=F32)
    lw_o[0] = -math.exp(-0.5) * _sigmoid(wl)
    a = _sigmoid(al)
    if has_vgate:
        v = v + (vf_ref[0].astype(F32) - v) * _sigmoid(vl)

    kk = k * kk_ref[...]
    kn = kk * lax.rsqrt(jnp.maximum(_headsum(kk * kk, bd_ref), 1e-24))
    k = k * (1.0 + (a - 1.0) * ka_ref[...])
    r_o[0] = r.astype(BF16)
    k_o[0] = k.astype(BF16)
    v_o[0] = v.astype(BF16)
    kn_o[0] = kn.astype(BF16)
    as_o[0] = a.astype(BF16)
    g_o[0] = g.astype(BF16)


def _rwkv_proj(x, gain, mu, w_rkv, w0, w1, w2, a0, a1, a2, g1, g2, k_k, k_a, v_gate, v_first):
    B, S, D = x.shape
    tm = TOKEN_TILE
    has_vgate = v_gate is not None
    row = lambda t: t.reshape(1, D)
    tile = pl.BlockSpec((1, tm, D), lambda b, i: (b, i, 0))
    prev = pl.BlockSpec((1, SUBLANES, D), lambda b, i: (b, jnp.maximum(i * (tm // SUBLANES) - 1, 0), 0))
    vec = pl.BlockSpec((1, D), lambda b, i: (0, 0))
    args = [x, x, row(gain), mu, w_rkv.astype(BF16), row(w0), w1.astype(BF16), w2.astype(BF16),
            row(a0), a1.astype(BF16), a2.astype(BF16), g1.astype(BF16), g2.astype(BF16),
            row(k_k), row(k_a), _block_diag_ones()]
    specs = [tile, prev, vec, _vmem_spec(), _vmem_spec(), vec, _vmem_spec(), _vmem_spec(),
             vec, _vmem_spec(), _vmem_spec(), _vmem_spec(), _vmem_spec(), vec, vec, _vmem_spec()]
    if has_vgate:
        v0, v1, v2 = v_gate
        args += [row(v0), v1.astype(BF16), v2.astype(BF16), v_first]
        specs += [vec, _vmem_spec(), _vmem_spec(), tile]
    bshape = jax.ShapeDtypeStruct((B, S, D), BF16)
    out_shape = [bshape, bshape, bshape, jax.ShapeDtypeStruct((B, S, D), F32), bshape, bshape, bshape]
    return pl.pallas_call(
        functools.partial(_rwkv_proj_kernel, has_vgate=has_vgate),
        grid=(B, S // tm),
        in_specs=specs,
        out_specs=[tile] * 7,
        out_shape=out_shape,
        compiler_params=_cparams(("parallel", "parallel")),
        name="rwkv_proj",
    )(*args)


def _rwkv_scan_kernel(r_ref, k_ref, v_ref, lw_ref, kn_ref, as_ref, tri_ref, y_ref, state_ref, *,
                      n_pairs, n_rows):
    C = CHUNK

    @pl.when(pl.program_id(1) == 0)
    def _():
        state_ref[...] = jnp.zeros_like(state_ref)

    tri = tri_ref[...]
    r_d, a_d, k_d, b_d, k_e, b_e, w_tot, v = ([] for _ in range(8))
    for bi in range(n_rows):
        lw = lw_ref[bi]
        lw_hi = lw.astype(BF16)
        rem = lw - lw_hi.astype(F32)
        lw_mid = rem.astype(BF16)
        lw_lo = (rem - lw_mid.astype(F32)).astype(BF16)
        cum = (jnp.dot(tri, lw_hi, preferred_element_type=F32) + jnp.dot(tri, lw_mid, preferred_element_type=F32)
               + jnp.dot(tri, lw_lo, preferred_element_type=F32))
        tot = cum[C - 1:C, :]
        r = r_ref[bi].astype(F32)
        k = k_ref[bi].astype(F32)
        kn = kn_ref[bi].astype(F32)
        b = kn * as_ref[bi].astype(F32)
        w_inv = jnp.exp(-cum)
        w_end = jnp.exp(tot - cum)
        r_d.append((r * jnp.exp(cum)).astype(BF16))
        a_d.append((-kn * jnp.exp(cum - lw)).astype(BF16))
        k_d.append((k * w_inv).astype(BF16))
        b_d.append((b * w_inv).astype(BF16))
        k_e.append((k * w_end).astype(BF16))
        b_e.append((b * w_end).astype(BF16))
        w_tot.append(jnp.exp(tot))
        v.append(v_ref[bi])

    ri = lax.broadcasted_iota(jnp.int32, (C, PAIR), 0)
    lane = lax.broadcasted_iota(jnp.int32, (C, PAIR), 1)
    even = lane < HEAD
    ci = jnp.where(even, lane, lane - HEAD)
    strict = ri > ci
    incl = ri >= ci
    eye = jnp.where(ri == ci, 1.0, 0.0).astype(F32)

    def bdiag(yp):
        yp = yp.astype(BF16)
        zero = jnp.zeros_like(yp)
        return jnp.concatenate([jnp.where(even, yp, zero), jnp.where(even, zero, yp)], axis=0)

    ch = [(bi, j) for bi in range(n_rows) for j in range(n_pairs)]
    n = range(len(ch))
    sls = [slice(j * PAIR, (j + 1) * PAIR) for _, j in ch]
    pick = lambda arrs, c: arrs[ch[c][0]][:, sls[c]]
    xa_r = [jnp.concatenate([pick(a_d, c), pick(r_d, c)], axis=0) for c in n]
    g = [_dot_nt(xa_r[c], jnp.concatenate([bdiag(pick(k_d, c)), bdiag(pick(b_d, c))], axis=0)) for c in n]
    a_ab = [jnp.where(strict, g[c][:C, PAIR:], 0.0) for c in n]
    a_kk = [jnp.concatenate([jnp.where(strict, g[c][:C, :PAIR], 0.0), jnp.where(incl, g[c][C:, :PAIR], 0.0)],
                            axis=0).astype(BF16) for c in n]
    a_rb = [jnp.where(incl, g[c][C:, PAIR:], 0.0).astype(BF16) for c in n]
    p = [eye + a_ab[c] for c in n]
    q = [_dot(a_ab[c], bdiag(a_ab[c])) for c in n]
    for _ in range(int(math.log2(C)) - 2):
        qp = [_dot(jnp.concatenate([q[c], p[c]], axis=0), bdiag(q[c])) for c in n]
        q = [qp[c][:C] for c in n]
        p = [p[c] + qp[c][C:] for c in n]
    p = [p[c] + _dot(p[c], bdiag(q[c])) for c in n]
    s0 = [state_ref[bi, j] for bi, j in ch]
    xs = [_dot_nt(xa_r[c], bdiag(s0[c])) for c in n]
    av = [_dot(a_kk[c], bdiag(pick(v, c))) for c in n]
    u = [_dot(p[c], bdiag(xs[c][:C] + av[c][:C])) for c in n]
    y = [xs[c][C:] + av[c][C:] + _dot(a_rb[c], bdiag(u[c])) for c in n]
    for c in n:
        bi, j = ch[c]
        y_ref[bi, :, sls[c]] = y[c].astype(y_ref.dtype)
        vu = jnp.concatenate([pick(v, c), u[c].astype(BF16)], axis=0)
        kb = jnp.concatenate([pick(k_e, c), pick(b_e, c)], axis=0)
        upd = _dot_tn(vu, kb)
        state_ref[bi, j] = s0[c] * pick(w_tot, c) + jnp.where(even, upd[:HEAD], upd[HEAD:])


def _rwkv_scan(r, k, v, lw, kn, a_sig):
    B, S, D = r.shape
    n_pairs = D // PAIR
    n_rows = math.gcd(B, SCAN_ROWS)
    tile = pl.BlockSpec((n_rows, CHUNK, D), lambda b, c: (b, c, 0))
    tri = jnp.asarray(np.tril(np.ones((CHUNK, CHUNK), np.float32)), dtype=BF16)
    return pl.pallas_call(
        functools.partial(_rwkv_scan_kernel, n_pairs=n_pairs, n_rows=n_rows),
        grid=(B // n_rows, S // CHUNK),
        in_specs=[tile] * 6 + [pl.BlockSpec((CHUNK, CHUNK), lambda b, c: (0, 0))],
        out_specs=tile,
        out_shape=jax.ShapeDtypeStruct((B, S, D), BF16),
        scratch_shapes=[pltpu.VMEM((n_rows, n_pairs, HEAD, PAIR), F32)],
        compiler_params=_cparams(("parallel", "arbitrary")),
        name="rwkv_scan",
    )(r, k, v, lw, kn, a_sig, tri)


def _rwkv_out_kernel(x_ref, y_ref, r_ref, k_ref, v_ref, g_ref, lnw_ref, lnb_ref, rk_ref, bd_ref, wo_ref,
                     fg_ref, win_ref, wout_ref, o_ref):
    y = y_ref[0].astype(F32)
    inv_n = 1.0 / HEAD
    mean = _headsum(y, bd_ref) * inv_n
    d = y - mean
    var = _headsum(d * d, bd_ref) * inv_n
    yn = d * lax.rsqrt(var + RWKV_LN_EPS) * lnw_ref[...] + lnb_ref[...]
    rk = r_ref[0].astype(F32) * k_ref[0].astype(F32) * rk_ref[...]
    bonus = _headsum(rk, bd_ref) * v_ref[0].astype(F32)
    z = ((yn + bonus) * g_ref[0].astype(F32)).astype(BF16)
    x2 = x_ref[0] + jnp.dot(z, wo_ref[...], preferred_element_type=F32)
    o_ref[0] = _ffn_tile(x2, fg_ref, win_ref, wout_ref)


def _rwkv_out(x, y, r, k, v, g, ln_w, ln_b, r_k, w_o, ffn_gain, w_in, w_out):
    B, S, D = x.shape
    tm = TOKEN_TILE
    tile = pl.BlockSpec((1, tm, D), lambda b, i: (b, i, 0))
    vec = pl.BlockSpec((1, D), lambda b, i: (0, 0))
    return pl.pallas_call(
        _rwkv_out_kernel,
        grid=(B, S // tm),
        in_specs=[tile] * 6 + [vec, vec, vec, _vmem_spec(), _vmem_spec(), vec, _vmem_spec(), _vmem_spec()],
        out_specs=tile,
        out_shape=jax.ShapeDtypeStruct((B, S, D), F32),
        compiler_params=_cparams(("parallel", "parallel")),
        name="rwkv_out_ffn",
    )(x, y, r, k, v, g, ln_w.reshape(1, D), ln_b.reshape(1, D), r_k.reshape(1, D),
      _block_diag_ones(), w_o.astype(BF16), ffn_gain.reshape(1, D), w_in.astype(BF16), w_out.astype(BF16))


def _attn_qkv_kernel(x_ref, gain_ref, w_ref, qg_ref, kg_ref, bd_ref, *refs, dm, dils):
    outs, h_scr = refs[:-1], refs[-1]
    tm = x_ref.shape[1]
    h = _rms(x_ref[0], gain_ref[...])
    n_col = h.shape[1] // LANES
    for c in range(n_col):
        h_scr[c] = h[:, c * LANES:(c + 1) * LANES]
    inv_n = 1.0 / HEAD
    for g, d in enumerate(dils):
        n = tm // d
        if d == 1:
            hp = h
        else:
            hp = jnp.concatenate(
                [jnp.concatenate([h_scr[c, pl.ds(r, n, stride=d), :] for c in range(n_col)], axis=1)
                 for r in range(d)], axis=0)
        qkv = jnp.dot(hp.astype(BF16), w_ref[:, g * 3 * dm:(g + 1) * 3 * dm], preferred_element_type=F32)
        q, k, v = qkv[:, :dm], qkv[:, dm:2 * dm], qkv[:, 2 * dm:]
        q = q * lax.rsqrt(_headsum(q * q, bd_ref) * inv_n + RMS_EPS) * qg_ref[g:g + 1, :]
        k = k * lax.rsqrt(_headsum(k * k, bd_ref) * inv_n + RMS_EPS) * kg_ref[g:g + 1, :]
        for t, o_ref in zip((q, k, v), outs[3 * g:3 * g + 3]):
            t = t.astype(BF16)
            for r in range(d):
                o_ref[0, r] = t[r * n:(r + 1) * n]


def _attn_qkv(x, gain, w, q_gain, k_gain):
    B, S, D = x.shape
    dm = w.shape[1] // (3 * len(DILATION_PAIRS))
    n_heads = dm // HEAD
    tm = TOKEN_TILE
    dils = tuple(d for _, d in DILATION_PAIRS)
    scale = HEAD ** -0.5 * LOG2E
    out_shape, out_specs = [], []
    for d in dils:
        for _ in range(3):
            out_shape.append(jax.ShapeDtypeStruct((B, d, S // d, dm), BF16))
            out_specs.append(pl.BlockSpec((1, d, tm // d, dm), lambda b, i: (b, 0, i, 0)))
    vec = lambda rows, n: pl.BlockSpec((rows, n), lambda b, i: (0, 0))
    return pl.pallas_call(
        functools.partial(_attn_qkv_kernel, dm=dm, dils=dils),
        grid=(B, S // tm),
        in_specs=[pl.BlockSpec((1, tm, D), lambda b, i: (b, i, 0)), vec(1, D), _vmem_spec(),
                  vec(len(dils), dm), vec(len(dils), dm), _vmem_spec()],
        out_specs=out_specs,
        out_shape=out_shape,
        scratch_shapes=[pltpu.VMEM((D // LANES, tm, LANES), F32)],
        compiler_params=_cparams(("parallel", "parallel")),
        name="attn_qkv",
    )(x, gain.reshape(1, D), w.astype(BF16), jnp.tile(q_gain, (1, n_heads)) * scale,
      jnp.tile(k_gain, (1, n_heads)), _block_diag_ones())


def _attn_core_kernel(q_ref, kp_ref, kc_ref, vp_ref, vc_ref, bias_ref, o_ref, lse_ref, *, n_heads, n_blk):
    q = q_ref[0, 0]
    k_all = jnp.concatenate([kp_ref[0, 0], kc_ref[0, 0]], axis=0)
    v_all = jnp.concatenate([vp_ref[0, 0], vc_ref[0, 0]], axis=0)
    first_tbl = jnp.minimum(pl.program_id(2), 1)
    lane_q = lax.broadcasted_iota(jnp.int32, (BAND, PAIR), 1)
    even = lane_q < HEAD
    lane = lax.broadcasted_iota(jnp.int32, (BAND, LSE_W), 1)
    zero = jnp.zeros((BAND, PAIR), BF16)

    def scores(t, pairs):
        tbl = first_tbl if t == 0 else 1
        logits = {}
        for j in pairs:
            sl = slice(j * PAIR, (j + 1) * PAIR)
            keys = k_all[t * BAND:(t + 2) * BAND, sl]
            qp = q[t * BAND:(t + 1) * BAND, sl]
            for hd, qm in ((2 * j, jnp.where(even, qp, zero)), (2 * j + 1, jnp.where(even, zero, qp))):
                s = lax.dot_general(qm, keys, (((1,), (1,)), ((), ())), preferred_element_type=F32)
                logits[hd] = s + bias_ref[tbl, hd]
        return logits

    def softmax_parts(logits):
        parts = {}
        for hd, s in logits.items():
            m = jnp.max(s, axis=-1, keepdims=True)
            e = jnp.exp2(s - m)
            parts[hd] = (m, jnp.sum(e, axis=-1, keepdims=True), e.astype(BF16))
        return parts

    def weighted_values(t, pairs, parts, m_pad, l_pad):
        for j in pairs:
            sl = slice(j * PAIR, (j + 1) * PAIR)
            vals = v_all[t * BAND:(t + 2) * BAND, sl]
            (m_e, l_e, p_e), (m_o, l_o, p_o) = parts[2 * j], parts[2 * j + 1]
            o_e = jnp.dot(p_e, vals, preferred_element_type=F32) / l_e
            o_o = jnp.dot(p_o, vals, preferred_element_type=F32) / l_o
            o_ref[0, 0, t * BAND:(t + 1) * BAND, sl] = jnp.where(even, o_e, o_o).astype(o_ref.dtype)
            for hd, m, l in ((2 * j, m_e, l_e), (2 * j + 1, m_o, l_o)):
                m_pad = jnp.where(lane == hd, m, m_pad)
                l_pad = jnp.where(lane == hd, l, l_pad)
        return m_pad, l_pad

    n_pairs = n_heads // 2
    units = [(t, range(j0, min(j0 + CORE_GROUP, n_pairs)))
             for t in range(n_blk) for j0 in range(0, n_pairs, CORE_GROUP)]
    m_pad = [jnp.zeros((BAND, LSE_W), F32)] * n_blk
    l_pad = [jnp.ones((BAND, LSE_W), F32)] * n_blk
    logits_q, parts_q = [], []
    for step in range(len(units) + 2):
        if step < len(units):
            logits_q.append(scores(*units[step]))
        if 1 <= step <= len(units):
            parts_q.append(softmax_parts(logits_q[step - 1]))
        if step >= 2:
            t, pairs = units[step - 2]
            m_pad[t], l_pad[t] = weighted_values(t, pairs, parts_q[step - 2], m_pad[t], l_pad[t])
    for t in range(n_blk):
        lse_ref[0, 0, t * BAND:(t + 1) * BAND] = (m_pad[t] + jnp.log2(l_pad[t])) * LN2


def _attn_core(q, k, v, bias, dilation):
    B, d, sd, dm = q.shape
    n_heads = dm // HEAD
    n_blk = math.gcd(sd // BAND, CORE_BLOCKS)
    rows = n_blk * BAND
    cur = lambda w: pl.BlockSpec((1, 1, rows, w), lambda b, r, n: (b, r, n, 0))
    prv = pl.BlockSpec((1, 1, BAND, dm), lambda b, r, n: (b, r, jnp.maximum(n * n_blk - 1, 0), 0))
    return pl.pallas_call(
        functools.partial(_attn_core_kernel, n_heads=n_heads, n_blk=n_blk),
        grid=(B, d, sd // rows),
        in_specs=[cur(dm), prv, cur(dm), prv, cur(dm), _vmem_spec()],
        out_specs=[cur(dm), cur(LSE_W)],
        out_shape=[jax.ShapeDtypeStruct((B, d, sd, dm), BF16),
                   jax.ShapeDtypeStruct((B, d, sd, LSE_W), F32)],
        compiler_params=_cparams(("parallel", "parallel", "parallel")),
        name=f"attn_core_d{dilation}",
    )(q, k, k, v, v, bias)


def _attn_out_kernel(x_ref, o1_ref, o2_ref, o3_ref, l1_ref, l2_ref, l3_ref, ex_ref, wo_ref,
                     fg_ref, win_ref, wout_ref, out_ref, o_scr, l_scr):
    tm = x_ref.shape[1]

    def token_order(ref, scr):
        d = ref.shape[1]
        if d == 1:
            return ref[0, 0].astype(F32)
        n_col = ref.shape[3] // LANES
        for r in range(d):
            blk = ref[0, r].astype(F32)
            for c in range(n_col):
                scr[c, pl.ds(r, tm // d, stride=d), :] = blk[:, c * LANES:(c + 1) * LANES]
        return jnp.concatenate([scr[c] for c in range(n_col)], axis=1)

    ls = [token_order(l_ref, l_scr.at[pl.ds(i, 1)]) for i, l_ref in enumerate((l1_ref, l2_ref, l3_ref))]
    m = jnp.maximum(jnp.maximum(ls[0], ls[1]), ls[2])
    es = [jnp.exp(l - m) for l in ls]
    inv = 1.0 / (es[0] + es[1] + es[2])

    def head_lanes(wgt):
        hi = wgt.astype(BF16)
        lo = (wgt - hi.astype(F32)).astype(BF16)
        return (jnp.dot(hi, ex_ref[...], preferred_element_type=F32)
                + jnp.dot(lo, ex_ref[...], preferred_element_type=F32))

    w1 = head_lanes(es[0] * inv)
    w2 = head_lanes(es[1] * inv)
    acc = w1 * token_order(o1_ref, o_scr)
    acc = acc + w2 * token_order(o2_ref, o_scr)
    acc = acc + (1.0 - w1 - w2) * token_order(o3_ref, o_scr)
    x2 = x_ref[0] + jnp.dot(acc.astype(BF16), wo_ref[...], preferred_element_type=F32)
    out_ref[0] = _ffn_tile(x2, fg_ref, win_ref, wout_ref)


def _attn_out(x, outs, lses, w_o, ffn_gain, w_in, w_out):
    B, S, D = x.shape
    dm = w_o.shape[0]
    n_heads = dm // HEAD
    tm = TOKEN_TILE
    expand = np.zeros((LSE_W, dm), np.float32)
    for hd in range(n_heads):
        expand[hd, hd * HEAD:(hd + 1) * HEAD] = 1.0
    tile = pl.BlockSpec((1, tm, D), lambda b, i: (b, i, 0))
    vec = pl.BlockSpec((1, D), lambda b, i: (0, 0))
    grouped = lambda t: pl.BlockSpec((1, t.shape[1], tm // t.shape[1], t.shape[3]), lambda b, i: (b, 0, i, 0))
    return pl.pallas_call(
        _attn_out_kernel,
        grid=(B, S // tm),
        in_specs=[tile] + [grouped(t) for t in outs] + [grouped(t) for t in lses]
                 + [_vmem_spec(), _vmem_spec(), vec, _vmem_spec(), _vmem_spec()],
        out_specs=tile,
        out_shape=jax.ShapeDtypeStruct((B, S, D), F32),
        scratch_shapes=[pltpu.VMEM((dm // LANES, tm, LANES), F32), pltpu.VMEM((len(lses), tm, LSE_W), F32)],
        compiler_params=_cparams(("parallel", "parallel")),
        name="attn_out_ffn",
    )(x, *outs, *lses, jnp.asarray(expand, dtype=BF16), w_o.astype(BF16),
      ffn_gain.reshape(1, D), w_in.astype(BF16), w_out.astype(BF16))


def _t5_bucket(dist):
    max_exact = NUM_BUCKETS // 2
    large = max_exact + (jnp.log(jnp.maximum(dist, 1).astype(F32) / max_exact)
                         / math.log(MAX_DISTANCE / max_exact) * (NUM_BUCKETS - max_exact)).astype(jnp.int32)
    large = jnp.minimum(large, NUM_BUCKETS - 1)
    return jnp.where(dist < max_exact, dist, large)


def _band_bias(table_g, window, dilation):
    steps = jnp.arange(3 * BAND - 1) - (BAND - 1)
    allowed = (steps >= 0) & (steps <= window // dilation)
    bucket = _t5_bucket(jnp.maximum(steps, 0) * dilation)
    by_step = jnp.where(allowed[:, None], jnp.take(table_g, bucket, axis=0).astype(F32) * LOG2E, NEG_INF)
    rev = by_step[::-1].T
    bias = jnp.stack([rev[:, BAND - 1 - i:3 * BAND - 1 - i] for i in range(BAND)], axis=1)
    first = jnp.where(jnp.arange(2 * BAND) >= BAND, bias, NEG_INF)
    return jnp.stack([first, bias])


def _attn_mixer(x, gain, w_qkv, q_gain, k_gain, w_o, rel_bias, ffn_gain, w_in, w_out):
    dm = w_o.shape[0]
    n_heads = dm // HEAD
    qkv = _attn_qkv(x, gain, w_qkv, q_gain, k_gain)
    outs, lses = [], []
    for g, (window, dilation) in enumerate(DILATION_PAIRS):
        q, k, v = qkv[3 * g:3 * g + 3]
        bias = _band_bias(rel_bias[:, g * n_heads:(g + 1) * n_heads], window, dilation)
        o, lse = _attn_core(q, k, v, bias, dilation)
        outs.append(o)
        lses.append(lse)
    return _attn_out(x, outs, lses, w_o, ffn_gain, w_in, w_out)


def kernel(x, mix_norm, ffn_norm, ffn_w_in, ffn_w_out, rwkv_mu, rwkv_w_rkv, rwkv_w0, rwkv_w1, rwkv_w2, rwkv_a0, rwkv_a1, rwkv_a2, rwkv_v0, rwkv_v1, rwkv_v2, rwkv_g1, rwkv_g2, rwkv_k_k, rwkv_k_a, rwkv_r_k, rwkv_ln_w, rwkv_ln_b, rwkv_w_o, attn_w_qkv, attn_q_gain, attn_k_gain, attn_w_o, rel_bias):
    depth = mix_norm.shape[0]
    v_first = None
    for layer in range(depth):
        i = layer // 2
        ffn = (ffn_norm[layer], ffn_w_in[layer], ffn_w_out[layer])
        if layer % 2 == 0:
            v_gate = None if i == 0 else (rwkv_v0[i - 1], rwkv_v1[i - 1], rwkv_v2[i - 1])
            r, k, v, lw, kn, a_sig, g = _rwkv_proj(
                x, mix_norm[layer], rwkv_mu[i], rwkv_w_rkv[i], rwkv_w0[i], rwkv_w1[i], rwkv_w2[i],
                rwkv_a0[i], rwkv_a1[i], rwkv_a2[i], rwkv_g1[i], rwkv_g2[i], rwkv_k_k[i], rwkv_k_a[i],
                v_gate, v_first)
            if v_first is None:
                v_first = v
            y = _rwkv_scan(r, k, v, lw, kn, a_sig)
            x = _rwkv_out(x, y, r, k, v, g, rwkv_ln_w[i], rwkv_ln_b[i], rwkv_r_k[i], rwkv_w_o[i], *ffn)
        else:
            x = _attn_mixer(x, mix_norm[layer], attn_w_qkv[i], attn_q_gain[i], attn_k_gain[i],
                            attn_w_o[i], rel_bias, *ffn)
    return x
```
